```python
import math
import jax
import jax.numpy as jnp
from jax import lax
import numpy as np

D_MODEL = 2048
BATCH = 8
SEQ = 2048
DEPTH = 2
DEC_BATCH = 128
DEC_SEQ = 1
PAST_LEN = 2048
PAGE_SIZE = 128

N_MIXERS = 2
N_A_LAYERS = (DEPTH + 1) // 2
N_B_LAYERS = DEPTH // 2
N_DENSE_LAYERS = (DEPTH + 1) // 2
N_MOE_LAYERS = DEPTH // 2

N_BUCKETS = 32
MAX_DISTANCE = 2048
N_BIAS_HEADS = 16

DIL_GROUPS = ((128, 1), (512, 4), (2048, 16))
N_GROUPS = len(DIL_GROUPS)
A_SLOTS = 16
A_HEAD_DIM = 64
A_QKV_WIDTH = 3 * N_GROUPS * A_SLOTS * A_HEAD_DIM
A_OUT_WIDTH = A_SLOTS * A_HEAD_DIM

B_HEADS = 16
B_KV_HEADS = 4
B_HEAD_DIM = 128
B_QKV_WIDTH = (B_HEADS + 2 * B_KV_HEADS) * B_HEAD_DIM
MOBA_BLOCK = 256
MOBA_TOPK = 3
MOBA_ROW_CHUNK = 32

D_FF = 5632
N_EXPERTS = 8
TOP_K = 2
D_FF_EXPERT = 5632
MOE_BLOCK = 256

RMS_EPS = 1e-6
NEG_INF = -1e30

kernel_name = 'hybrid_dilated_moba_decoder_step'


def rmsnorm(x, g):
    xf = x.astype(jnp.float32)
    y = xf * lax.rsqrt(jnp.mean(xf * xf, axis=-1, keepdims=True) + RMS_EPS)
    return (y * g.astype(jnp.float32)).astype(x.dtype)


def t5_bucket(dist):
    max_exact = N_BUCKETS // 2
    n = jnp.maximum(dist, 0)
    nf = jnp.maximum(n, 1).astype(jnp.float32)
    scale = (N_BUCKETS - max_exact) / math.log(MAX_DISTANCE / max_exact)
    large = max_exact + (jnp.log(nf / max_exact) * scale).astype(jnp.int32)
    return jnp.where(n < max_exact, n, jnp.minimum(large, N_BUCKETS - 1))


def _softmax_lse(logits, mask):
    logits = jnp.where(mask, logits, NEG_INF)
    m = jnp.max(logits, axis=-1, keepdims=True)
    e = jnp.exp(logits - m)
    l = jnp.sum(e, axis=-1, keepdims=True)
    return e / l, (m + jnp.log(l))[..., 0]


def swiglu(x, w_gate, w_up, w_down):
    return (jax.nn.silu(x @ w_gate) * (x @ w_up)) @ w_down


def _dilated_group_prompt(q, k, v, window, dil, bias_table):
    B, S, H, Dh = q.shape
    w = window // dil
    L = S // dil
    nb = -(-L // w)
    Lp = nb * w

    def to_streams(t, front):
        t = t.reshape(B, L, dil, H, Dh).transpose(0, 2, 1, 3, 4)
        return jnp.pad(t, ((0, 0), (0, 0), (front, Lp - L), (0, 0), (0, 0)))

    qs = to_streams(q, 0).reshape(B, dil, nb, w, H, Dh)
    ks = to_streams(k, w)
    vs = to_streams(v, w)

    def band(t):
        prev = t[:, :, :Lp].reshape(B, dil, nb, w, H, Dh)
        cur = t[:, :, w:].reshape(B, dil, nb, w, H, Dh)
        return jnp.concatenate([prev, cur], axis=3)

    kb, vb = band(ks), band(vs)
    i = jnp.arange(w)[:, None]
    j = jnp.arange(2 * w)[None, :]
    delta = i + w - j
    n = jnp.arange(nb)[:, None, None]
    mask = (delta >= 0) & (delta <= w) & ((n > 0) | (j >= w))
    bias = bias_table.astype(jnp.float32)[t5_bucket(delta * dil)].transpose(2, 0, 1)
    logits = jnp.einsum('brnqhd,brnkhd->brnhqk', qs, kb, preferred_element_type=jnp.float32)
    logits = logits * (Dh ** -0.5) + bias[None, None, None]
    p, lse = _softmax_lse(logits, mask[None, None, :, None])
    o = jnp.einsum('brnhqk,brnkhd->brnqhd', p, vb.astype(jnp.float32))
    o = o.reshape(B, dil, Lp, H, Dh)[:, :, :L].transpose(0, 2, 1, 3, 4).reshape(B, S, H, Dh)
    lse = lse.transpose(0, 1, 2, 4, 3).reshape(B, dil, Lp, H)[:, :, :L]
    lse = lse.transpose(0, 2, 1, 3).reshape(B, S, H)
    return o, lse


def _dilated_group_sample(q, k_ext, v_ext, buf_len, window, dil, bias_table):
    T = q.shape[1]
    dist = jnp.arange(window // dil + 1) * dil
    idx = buf_len + jnp.arange(T)[:, None] - dist[None, :]
    mask = idx >= 0
    idx = jnp.maximum(idx, 0)
    kg = k_ext[:, idx]
    vg = v_ext[:, idx]
    bias = bias_table.astype(jnp.float32)[t5_bucket(dist)].T
    logits = jnp.einsum('bthd,btkhd->bthk', q, kg, preferred_element_type=jnp.float32)
    logits = logits * (q.shape[-1] ** -0.5) + bias[None, None]
    p, lse = _softmax_lse(logits, mask[None, :, None, :])
    o = jnp.einsum('bthk,btkhd->bthd', p, vg.astype(jnp.float32))
    return o, lse


def _merge_groups(outs, lses):
    wts = jax.nn.softmax(jnp.stack(lses, axis=0), axis=0)
    return jnp.sum(wts[..., None] * jnp.stack(outs, axis=0), axis=0)


def dilated_mixer_prompt(h, w_qkv, w_o, bias_table):
    B, S, _ = h.shape
    qkv = (h @ w_qkv).reshape(B, S, 3, N_GROUPS, A_SLOTS, A_HEAD_DIM)
    outs, lses, new_kv = [], [], []
    for g, (window, dil) in enumerate(DIL_GROUPS):
        q, k, v = qkv[:, :, 0, g], qkv[:, :, 1, g], qkv[:, :, 2, g]
        o, lse = _dilated_group_prompt(q, k, v, window, dil, bias_table)
        outs.append(o)
        lses.append(lse)
        keep = min(window, S)
        new_kv.append(jnp.stack([k[:, S - keep:], v[:, S - keep:]], axis=2))
    o = _merge_groups(outs, lses).reshape(B, S, A_OUT_WIDTH).astype(h.dtype)
    return o @ w_o, new_kv


def dilated_mixer_sample(h, bufs, w_qkv, w_o, bias_table):
    B, T, _ = h.shape
    qkv = (h @ w_qkv).reshape(B, T, 3, N_GROUPS, A_SLOTS, A_HEAD_DIM)
    outs, lses, new_rows = [], [], []
    for g, (window, dil) in enumerate(DIL_GROUPS):
        q, k, v = qkv[:, :, 0, g], qkv[:, :, 1, g], qkv[:, :, 2, g]
        buf = bufs[g]
        k_ext = jnp.concatenate([buf[:, :, 0], k], axis=1)
        v_ext = jnp.concatenate([buf[:, :, 1], v], axis=1)
        o, lse = _dilated_group_sample(q, k_ext, v_ext, buf.shape[1], window, dil, bias_table)
        outs.append(o)
        lses.append(lse)
        new_rows.append(jnp.stack([k, v], axis=2))
    o = _merge_groups(outs, lses).reshape(B, T, A_OUT_WIDTH).astype(h.dtype)
    return o @ w_o, new_rows


def _moba_attention(q, k, v, pos0, bias_table):
    B, Tq, H, D = q.shape
    Tk, KVH = k.shape[1], k.shape[2]
    G = H // KVH
    nblk = -(-Tk // MOBA_BLOCK)
    pad = nblk * MOBA_BLOCK - Tk

    def blockify(t):
        t = jnp.pad(t, ((0, 0), (0, pad), (0, 0), (0, 0)))
        return t.reshape(B, nblk, MOBA_BLOCK, KVH, D).transpose(0, 3, 1, 2, 4)

    kblk, vblk = blockify(k), blockify(v)
    kmean = jnp.mean(kblk.astype(jnp.float32), axis=3)
    qpos = pos0 + jnp.arange(Tq)
    qblk = qpos // MOBA_BLOCK
    gate = jnp.einsum('btkgd,bknd->btkgn', q.reshape(B, Tq, KVH, G, D).astype(jnp.float32), kmean)
    gate = gate.reshape(B, Tq, H, nblk)
    fully_past = jnp.arange(nblk)[None, :] < qblk[:, None]
    gate = jnp.where(fully_past[None, :, None, :], gate, NEG_INF)
    n_sel = min(MOBA_TOPK, nblk)
    _, sel = lax.top_k(gate, n_sel)
    sel_ok = sel < qblk[None, :, None, None]
    own = jnp.broadcast_to(jnp.minimum(qblk, nblk - 1)[None, :, None, None], (B, Tq, H, 1))
    blocks = jnp.concatenate([sel, own], axis=-1).astype(jnp.int32)
    blk_ok = jnp.concatenate([sel_ok, jnp.ones(own.shape, bool)], axis=-1)
    nk_blk = n_sel + 1

    R = B * Tq
    rc = min(MOBA_ROW_CHUNK, R)
    nr = -(-R // rc)
    rpad = nr * rc - R

    def rows(t):
        t = jnp.pad(t, [(0, rpad)] + [(0, 0)] * (t.ndim - 1), mode='edge')
        return t.reshape((nr, rc) + t.shape[1:])

    q_r = rows(q.reshape(R, H, D))
    blk_r = rows(blocks.reshape(R, H, nk_blk))
    ok_r = rows(blk_ok.reshape(R, H, nk_blk))
    b_r = rows(jnp.repeat(jnp.arange(B), Tq))
    pos_r = rows(jnp.tile(qpos, B))
    hkv = (jnp.arange(H) // G)[None, :, None]
    hsel = jnp.arange(H)[None, :, None]
    table_t = bias_table.astype(jnp.float32).T
    kdim = nk_blk * MOBA_BLOCK

    def attend(args):
        q_c, blk_c, ok_c, b_c, pos_c = args
        kg = kblk[b_c[:, None, None], hkv, blk_c].reshape(rc, H, kdim, D)
        vg = vblk[b_c[:, None, None], hkv, blk_c].reshape(rc, H, kdim, D)
        kpos = blk_c[..., None] * MOBA_BLOCK + jnp.arange(MOBA_BLOCK)
        mask = (ok_c[..., None] & (kpos <= pos_c[:, None, None, None])).reshape(rc, H, kdim)
        kpos = kpos.reshape(rc, H, kdim)
        bias = table_t[hsel, t5_bucket(pos_c[:, None, None] - kpos)]
        logits = jnp.einsum('rhd,rhkd->rhk', q_c, kg, preferred_element_type=jnp.float32)
        logits = logits * (D ** -0.5) + bias
        p, _ = _softmax_lse(logits, mask)
        return jnp.einsum('rhk,rhkd->rhd', p, vg.astype(jnp.float32))

    o = lax.map(attend, (q_r, blk_r, ok_r, b_r, pos_r))
    return o.reshape(nr * rc, H, D)[:R].reshape(B, Tq, H, D)


def moba_mixer(h, w_qkv, w_o, bias_table, k_past, v_past, pos0):
    B, T, _ = h.shape
    qd, kd = B_HEADS * B_HEAD_DIM, B_KV_HEADS * B_HEAD_DIM
    qkv = h @ w_qkv
    q = qkv[..., :qd].reshape(B, T, B_HEADS, B_HEAD_DIM)
    k = qkv[..., qd:qd + kd].reshape(B, T, B_KV_HEADS, B_HEAD_DIM)
    v = qkv[..., qd + kd:].reshape(B, T, B_KV_HEADS, B_HEAD_DIM)
    k_all = k if k_past is None else jnp.concatenate([k_past, k], axis=1)
    v_all = v if v_past is None else jnp.concatenate([v_past, v], axis=1)
    o = _moba_attention(q, k_all, v_all, pos0, bias_table)
    return o.reshape(B, T, qd).astype(h.dtype) @ w_o, jnp.stack([k, v], axis=2)


def moe_ffn(x, w_router, w_gate, w_up, w_down):
    N, D = x.shape
    logits = x.astype(jnp.float32) @ w_router.astype(jnp.float32)
    top_val, top_idx = lax.top_k(logits, TOP_K)
    gates = jax.nn.softmax(top_val, axis=-1)
    e_flat = top_idx.reshape(-1)
    tok_flat = jnp.arange(N * TOP_K) // TOP_K
    g_flat = gates.reshape(-1)
    order = jnp.argsort(e_flat)
    e_sorted = e_flat[order]
    counts = jnp.bincount(e_flat, length=N_EXPERTS)
    padded = (counts + MOE_BLOCK - 1) // MOE_BLOCK * MOE_BLOCK
    pad_end = jnp.cumsum(padded)
    pad_start = pad_end - padded
    raw_start = jnp.cumsum(counts) - counts
    dest = pad_start[e_sorted] + (jnp.arange(N * TOP_K) - raw_start[e_sorted])
    n_rows = -(-(N * TOP_K + N_EXPERTS * (MOE_BLOCK - 1)) // MOE_BLOCK) * MOE_BLOCK
    row_tok = jnp.full((n_rows,), N, jnp.int32).at[dest].set(tok_flat[order].astype(jnp.int32))
    row_gate = jnp.zeros((n_rows,), jnp.float32).at[dest].set(g_flat[order])
    n_blocks = n_rows // MOE_BLOCK
    blk_expert = jnp.searchsorted(pad_end, jnp.arange(n_blocks) * MOE_BLOCK, side='right')
    blk_expert = jnp.minimum(blk_expert, N_EXPERTS - 1)
    x_pad = jnp.concatenate([x, jnp.zeros((1, D), x.dtype)], axis=0)
    xb = x_pad[row_tok].reshape(n_blocks, MOE_BLOCK, D)

    def expert_block(args):
        xe, e = args
        return swiglu(xe, w_gate[e], w_up[e], w_down[e])

    yb = lax.map(expert_block, (xb, blk_expert)).reshape(n_rows, D)
    y = jax.ops.segment_sum(yb.astype(jnp.float32) * row_gate[:, None], row_tok, num_segments=N + 1)[:N]
    return y.astype(x.dtype)


def setup_inputs(seed: int = 0) -> dict:
    key = jax.random.key(seed)
    keys = iter(jax.random.split(key, 32))
    f32 = jnp.float32

    def nrm(shape, scale=1.0):
        return jax.random.normal(next(keys), shape, f32) * scale

    n_pages = PAST_LEN // PAGE_SIZE
    n_used = DEC_BATCH * n_pages
    n_pool = n_used + max(1, n_used // 4)
    x_prompt = nrm((BATCH, SEQ, D_MODEL))
    x_sample = nrm((DEC_BATCH, DEC_SEQ, D_MODEL))
    win = [nrm((N_A_LAYERS, DEC_BATCH, min(wnd, PAST_LEN), 2, A_SLOTS, A_HEAD_DIM)) for wnd, _ in DIL_GROUPS]
    cache_moba_kv = nrm((N_B_LAYERS, n_pool, PAGE_SIZE, 2, B_KV_HEADS, B_HEAD_DIM))
    page_table = jax.random.permutation(next(keys), n_pool)[:n_used].reshape(DEC_BATCH, n_pages).astype(jnp.int32)
    return {
        'x_prompt': x_prompt,
        'x_sample': x_sample,
        'cache_win0_kv': win[0],
        'cache_win1_kv': win[1],
        'cache_win2_kv': win[2],
        'cache_moba_kv': cache_moba_kv,
        'page_table': page_table,
        'rel_bias': nrm((N_BUCKETS, N_BIAS_HEADS), 0.5),
        'norm_mix': 1.0 + nrm((DEPTH, D_MODEL), 0.02),
        'norm_ffn': 1.0 + nrm((DEPTH, D_MODEL), 0.02),
        'norm_final': 1.0 + nrm((D_MODEL,), 0.02),
        'w_qkv_dil': nrm((N_A_LAYERS, D_MODEL, A_QKV_WIDTH), D_MODEL ** -0.5),
        'w_o_dil': nrm((N_A_LAYERS, A_OUT_WIDTH, D_MODEL), A_OUT_WIDTH ** -0.5),
        'w_qkv_moba': nrm((N_B_LAYERS, D_MODEL, B_QKV_WIDTH), D_MODEL ** -0.5),
        'w_o_moba': nrm((N_B_LAYERS, B_HEADS * B_HEAD_DIM, D_MODEL), (B_HEADS * B_HEAD_DIM) ** -0.5),
        'w_gate_dense': nrm((N_DENSE_LAYERS, D_MODEL, D_FF), D_MODEL ** -0.5),
        'w_up_dense': nrm((N_DENSE_LAYERS, D_MODEL, D_FF), D_MODEL ** -0.5),
        'w_down_dense': nrm((N_DENSE_LAYERS, D_FF, D_MODEL), D_FF ** -0.5),
        'w_router': nrm((N_MOE_LAYERS, D_MODEL, N_EXPERTS), D_MODEL ** -0.5),
        'w_gate_moe': nrm((N_MOE_LAYERS, N_EXPERTS, D_MODEL, D_FF_EXPERT), D_MODEL ** -0.5),
        'w_up_moe': nrm((N_MOE_LAYERS, N_EXPERTS, D_MODEL, D_FF_EXPERT), D_MODEL ** -0.5),
        'w_down_moe': nrm((N_MOE_LAYERS, N_EXPERTS, D_FF_EXPERT, D_MODEL), D_FF_EXPERT ** -0.5),
    }


def reference(x_prompt, x_sample, cache_win0_kv, cache_win1_kv, cache_win2_kv, cache_moba_kv, page_table,
              rel_bias, norm_mix, norm_ffn, norm_final, w_qkv_dil, w_o_dil, w_qkv_moba, w_o_moba,
              w_gate_dense, w_up_dense, w_down_dense, w_router, w_gate_moe, w_up_moe, w_down_moe):
    win_caches = (cache_win0_kv, cache_win1_kv, cache_win2_kv)
    xp, xs = x_prompt, x_sample
    Bp, Sp, _ = xp.shape
    Bd = page_table.shape[0]
    win_p = [[] for _ in range(N_GROUPS)]
    win_s = [[] for _ in range(N_GROUPS)]
    moba_p_list, moba_s_list = [], []
    for i in range(DEPTH):
        li = i // N_MIXERS
        hp = rmsnorm(xp, norm_mix[i])
        hs = rmsnorm(xs, norm_mix[i])
        if i % N_MIXERS == 0:
            mp, kv_p = dilated_mixer_prompt(hp, w_qkv_dil[li], w_o_dil[li], rel_bias)
            ms, kv_s = dilated_mixer_sample(hs, [c[li] for c in win_caches], w_qkv_dil[li], w_o_dil[li], rel_bias)
            for g in range(N_GROUPS):
                win_p[g].append(kv_p[g])
                win_s[g].append(kv_s[g])
        else:
            past = cache_moba_kv[li][page_table]
            past = past.reshape(Bd, -1, 2, B_KV_HEADS, B_HEAD_DIM)
            mp, kvp = moba_mixer(hp, w_qkv_moba[li], w_o_moba[li], rel_bias, None, None, 0)
            ms, kvs = moba_mixer(hs, w_qkv_moba[li], w_o_moba[li], rel_bias, past[:, :, 0], past[:, :, 1], past.shape[1])
            moba_p_list.append(kvp.reshape(Bp, Sp // PAGE_SIZE, PAGE_SIZE, 2, B_KV_HEADS, B_HEAD_DIM))
            moba_s_list.append(kvs)
        xp = xp + mp
        xs = xs + ms
        ci = i // 2
        hp = rmsnorm(xp, norm_ffn[i])
        hs = rmsnorm(xs, norm_ffn[i])
        if i % 2 == 0:
            fp = swiglu(hp, w_gate_dense[ci], w_up_dense[ci], w_down_dense[ci])
            fs = swiglu(hs, w_gate_dense[ci], w_up_dense[ci], w_down_dense[ci])
        else:
            fp = moe_ffn(hp.reshape(-1, D_MODEL), w_router[ci], w_gate_moe[ci], w_up_moe[ci], w_down_moe[ci]).reshape(hp.shape)
            fs = moe_ffn(hs.reshape(-1, D_MODEL), w_router[ci], w_gate_moe[ci], w_up_moe[ci], w_down_moe[ci]).reshape(hs.shape)
        xp = xp + fp
        xs = xs + fs
    y_prompt = rmsnorm(xp, norm_final)
    y_sample = rmsnorm(xs, norm_final)
    win0_p, win1_p, win2_p = [jnp.stack(l, axis=0) for l in win_p]
    win0_s, win1_s, win2_s = [jnp.stack(l, axis=0) for l in win_s]
    moba_p = jnp.stack(moba_p_list, axis=0)
    moba_s = jnp.stack(moba_s_list, axis=0)
    return (y_prompt, y_sample, win0_p, win1_p, win2_p, win0_s, win1_s, win2_s, moba_p, moba_s)
```

```python
import functools
import math

import numpy as np
import jax
import jax.numpy as jnp
from jax import lax
from jax.experimental import pallas as pl
from jax.experimental.pallas import tpu as pltpu

F32 = jnp.float32
BF16 = jnp.bfloat16
I32 = jnp.int32

RMS_EPS = 1e-6
NEG_INF = -1e30
MIB = 1024 * 1024

N_BUCKETS = 32
MAX_DISTANCE = 2048
DIL_GROUPS = ((128, 1), (512, 4), (2048, 16))
A_SLOTS = 16
A_HEAD_DIM = 64
B_HEADS = 16
B_KV_HEADS = 4
B_HEAD_DIM = 128
MOBA_BLOCK = 256
MOBA_TOPK = 3
PAGE_SIZE = 128
N_EXPERTS = 8
TOP_K = 2

LANES = 128

ROW_TILE = 688
MOE_ROW_TILE = 512
FF_TILE = 512
COMBINE_TILE = 384


def _params(sem, vmem_mib):
    return pltpu.CompilerParams(dimension_semantics=sem, vmem_limit_bytes=vmem_mib * MIB)


def _bucket_np(dist):
    max_exact = N_BUCKETS // 2
    n = np.maximum(np.asarray(dist, np.int64), 0)
    nf = np.maximum(n, 1).astype(np.float32)
    scale = np.float32((N_BUCKETS - max_exact) / math.log(MAX_DISTANCE / max_exact))
    large = max_exact + (np.log(nf / np.float32(max_exact)) * scale).astype(np.int32)
    return np.where(n < max_exact, n, np.minimum(large, N_BUCKETS - 1)).astype(np.int32)


def _split3(t):
    t1 = t.astype(BF16)
    r1 = t - t1.astype(F32)
    t2 = r1.astype(BF16)
    r2 = r1 - t2.astype(F32)
    return t1, t2, r2.astype(BF16)


def _dot(a, b):
    return jnp.dot(a, b, preferred_element_type=F32)


def _dot_nt(a, b):
    return lax.dot_general(a, b, (((1,), (1,)), ((), ())), preferred_element_type=F32)


def _dot_sel_right(a_f32, sel_bf16):
    a1 = a_f32.astype(BF16)
    a2 = (a_f32 - a1.astype(F32)).astype(BF16)
    return _dot(a1, sel_bf16) + _dot(a2, sel_bf16)


def _dot_sel_left(sel_bf16, t_f32):
    t1, t2, t3 = _split3(t_f32)
    return _dot(sel_bf16, t1) + _dot(sel_bf16, t2) + _dot(sel_bf16, t3)


def _matmul(x, w, gamma=None, residual=None, *, tm, tn):
    m, k = x.shape
    n = w.shape[1]
    prenorm = gamma is not None
    stage = prenorm or x.dtype != BF16
    has_res = residual is not None

    def body(*refs):
        it = iter(refs)
        x_ref = next(it)
        g_ref = next(it) if prenorm else None
        w_ref = next(it)
        r_ref = next(it) if has_res else None
        o_ref = next(it)
        hb_ref = next(it) if stage else None
        if stage:
            @pl.when(pl.program_id(1) == 0)
            def _():
                xv = x_ref[...].astype(F32)
                if prenorm:
                    xv = xv * lax.rsqrt(jnp.mean(xv * xv, axis=-1, keepdims=True) + RMS_EPS) * g_ref[...]
                hb_ref[...] = xv.astype(BF16)
            a = hb_ref[...]
        else:
            a = x_ref[...]
        acc = _dot(a, w_ref[...])
        if has_res:
            acc = acc + r_ref[...]
        o_ref[...] = acc

    in_specs = [pl.BlockSpec((tm, k), lambda i, j: (i, 0))]
    args = [x]
    if prenorm:
        in_specs.append(pl.BlockSpec((1, k), lambda i, j: (0, 0)))
        args.append(gamma.reshape(1, k).astype(F32))
    in_specs.append(pl.BlockSpec((k, tn), lambda i, j: (0, j)))
    args.append(w)
    if has_res:
        in_specs.append(pl.BlockSpec((tm, tn), lambda i, j: (i, j)))
        args.append(residual)
    scratch = [pltpu.VMEM((tm, k), BF16)] if stage else []
    vmem = 2 * (tm * k * x.dtype.itemsize + k * tn * 2 + tm * tn * 4 * (2 if has_res else 1)) + tm * k * 2
    return pl.pallas_call(
        body,
        grid=(m // tm, n // tn),
        in_specs=in_specs,
        out_specs=pl.BlockSpec((tm, tn), lambda i, j: (i, j)),
        out_shape=jax.ShapeDtypeStruct((m, n), F32),
        scratch_shapes=scratch,
        compiler_params=_params(("arbitrary", "arbitrary"), vmem // MIB + 8),
    )(*args)


def _ffn(x, wg, wu, wd, tile_expert, n_used, gamma=None, row_gate=None, *, add_res, tm, tf):
    m, d = x.shape
    ff = wg.shape[2]
    nj = ff // tf
    n_tiles = m // tm
    prenorm = gamma is not None
    use_gate = row_gate is not None

    def body(te_ref, nu_ref, *refs):
        it = iter(refs)
        x_ref = next(it)
        g_ref = next(it) if prenorm else None
        wg_ref, wu_ref, wd_ref = next(it), next(it), next(it)
        rg_ref = next(it) if use_gate else None
        o_ref = next(it)
        hb_ref = next(it)
        i = pl.program_id(0)
        j = pl.program_id(1)
        active = i < nu_ref[0]

        @pl.when(jnp.logical_and(active, j == 0))
        def _():
            xv = x_ref[...]
            if prenorm:
                xv = xv * lax.rsqrt(jnp.mean(xv * xv, axis=-1, keepdims=True) + RMS_EPS) * g_ref[...]
            hb_ref[...] = xv.astype(BF16)

        @pl.when(j == 0)
        def _():
            o_ref[...] = jnp.zeros_like(o_ref)

        @pl.when(active)
        def _():
            h = hb_ref[...]
            gt = _dot(h, wg_ref[0])
            up = _dot(h, wu_ref[0])
            a = (gt / (1.0 + jnp.exp(-gt)) * up).astype(BF16)
            o_ref[...] += _dot(a, wd_ref[0])

        @pl.when(jnp.logical_and(active, j == nj - 1))
        def _():
            r = o_ref[...]
            if use_gate:
                r = r * rg_ref[...]
            if add_res:
                r = r + x_ref[...]
            o_ref[...] = r

    def row_map(i, j, te, nu):
        return (jnp.minimum(i, nu[0] - 1), 0)

    def col_of(i, j, nu):
        return jnp.where(i < nu[0], j, nj - 1)

    def exp_of(i, te, nu):
        return te[jnp.minimum(i, nu[0] - 1)]

    in_specs = [pl.BlockSpec((tm, d), row_map)]
    args = [x]
    if prenorm:
        in_specs.append(pl.BlockSpec((1, d), lambda i, j, te, nu: (0, 0)))
        args.append(gamma.reshape(1, d).astype(F32))
    in_specs += [
        pl.BlockSpec((1, d, tf), lambda i, j, te, nu: (exp_of(i, te, nu), 0, col_of(i, j, nu))),
        pl.BlockSpec((1, d, tf), lambda i, j, te, nu: (exp_of(i, te, nu), 0, col_of(i, j, nu))),
        pl.BlockSpec((1, tf, d), lambda i, j, te, nu: (exp_of(i, te, nu), col_of(i, j, nu), 0)),
    ]
    args += [wg, wu, wd]
    if use_gate:
        in_specs.append(pl.BlockSpec((tm, 1), row_map))
        args.append(row_gate.reshape(m, 1))
    vmem = 2 * (tm * d * 4 + 3 * d * tf * 2 + tm * d * 4 + (tm * LANES * 4 if use_gate else 0)) + tm * d * 2
    return pl.pallas_call(
        body,
        grid_spec=pltpu.PrefetchScalarGridSpec(
            num_scalar_prefetch=2,
            grid=(n_tiles, nj),
            in_specs=in_specs,
            out_specs=pl.BlockSpec((tm, d), lambda i, j, te, nu: (i, 0)),
            scratch_shapes=[pltpu.VMEM((tm, d), BF16)],
        ),
        out_shape=jax.ShapeDtypeStruct((m, d), F32),
        compiler_params=_params(("arbitrary", "arbitrary"), vmem // MIB + 10),
    )(tile_expert, n_used, *args)


def _dil_bucket_maps(w, dil):
    i = np.arange(w)[:, None]
    j = np.arange(2 * w)[None, :]
    delta = i + w - j
    valid = (delta >= 0) & (delta <= w)
    bk = np.where(valid, _bucket_np(delta * dil), -1).astype(np.int32)
    first = np.where(j >= w, bk, -1).astype(np.int32)
    return np.stack([first, bk], axis=0)


def _dil_prompt_group(qkv, table, g, window, dil, *, batch, seq, heads, head_dim):
    hd = heads * head_dim
    n_rows = qkv.shape[0]
    w = window // dil
    length = seq // dil
    nb = length // w
    assert length % w == 0 and n_rows % dil == 0 and heads % 2 == 0 and 2 * head_dim == LANES
    n_cols = qkv.shape[1] // hd
    view = qkv.reshape(n_rows // dil, dil * n_cols * hd)
    bkt = jnp.asarray(_dil_bucket_maps(w, dil))
    scale = head_dim ** -0.5

    def body(table_ref, bkt_ref, q_ref, kp_ref, kc_ref, vp_ref, vc_ref, o_ref, lse_ref, bias_ref):
        first_step = (pl.program_id(0) == 0) & (pl.program_id(1) == 0) & (pl.program_id(2) == 0)

        @pl.when(first_step)
        def _():
            def per_head(h, carry):
                for v in range(2):
                    bk = bkt_ref[v]
                    t = lax.fori_loop(0, N_BUCKETS, lambda k, t: jnp.where(bk == k, table_ref[k, h], t),
                                      jnp.zeros(bk.shape, F32))
                    bias_ref[v, h] = jnp.where(bk < 0, NEG_INF, t)
                return carry
            lax.fori_loop(0, heads, per_head, 0)

        variant = (pl.program_id(2) > 0).astype(I32)
        lane = lax.broadcasted_iota(I32, (w, LANES), 1)
        low = lane < head_dim
        for hp in range(heads // 2):
            sl = slice(hp * LANES, (hp + 1) * LANES)
            q2 = q_ref[:, sl] * scale
            k2 = jnp.concatenate([kp_ref[:, sl], kc_ref[:, sl]], axis=0).astype(BF16)
            v2 = jnp.concatenate([vp_ref[:, sl], vc_ref[:, sl]], axis=0).astype(BF16)
            outs, lses = [], []
            for half in range(2):
                keep = low if half == 0 else jnp.logical_not(low)
                qm = jnp.where(keep, q2, 0.0).astype(BF16)
                s = _dot_nt(qm, k2) + bias_ref[variant, 2 * hp + half]
                mx = jnp.max(s, axis=-1, keepdims=True)
                e = jnp.exp(s - mx)
                den = jnp.sum(e, axis=-1, keepdims=True)
                outs.append(_dot(e.astype(BF16), v2) / den)
                lses.append(mx + jnp.log(den))
            o_ref[:, sl] = jnp.where(low, outs[0], outs[1])
            lse_ref[:, sl] = jnp.where(low, lses[0], lses[1])

    def blk(col_off, prev):
        def index(b, r, n):
            nn = jnp.maximum(n - 1, 0) if prev else n
            return (b * nb + nn, r * n_cols + col_off)
        return pl.BlockSpec((w, hd), index)

    n_g = n_cols // 3
    out_spec = pl.BlockSpec((w, hd), lambda b, r, n: (b * nb + n, r))
    out_sds = jax.ShapeDtypeStruct((batch * length, dil * hd), F32)
    o, lse = pl.pallas_call(
        body,
        grid=(batch, dil, nb),
        in_specs=[
            pl.BlockSpec(memory_space=pltpu.SMEM),
            pl.BlockSpec((2, w, 2 * w), lambda b, r, n: (0, 0, 0)),
            blk(g, False), blk(n_g + g, True), blk(n_g + g, False), blk(2 * n_g + g, True), blk(2 * n_g + g, False),
        ],
        out_specs=[out_spec, out_spec],
        out_shape=[out_sds, out_sds],
        scratch_shapes=[pltpu.VMEM((2, heads, w, 2 * w), F32)],
        compiler_params=_params(("arbitrary", "arbitrary", "arbitrary"), 32),
    )(table, bkt, view, view, view, view, view)
    return o.reshape(batch * seq, hd), lse.reshape(batch * seq, hd)


def _merge_groups(outs, lses, *, tm):
    m, hd = outs[0].shape
    ng = len(outs)

    def body(*refs):
        o_refs, l_refs, out_ref = refs[:ng], refs[ng:2 * ng], refs[2 * ng]
        ls = [r[...] for r in l_refs]
        mx = functools.reduce(jnp.maximum, ls)
        ws = [jnp.exp(l - mx) for l in ls]
        num = functools.reduce(lambda a, b: a + b, [wt * r[...] for wt, r in zip(ws, o_refs)])
        out_ref[...] = (num / functools.reduce(lambda a, b: a + b, ws)).astype(out_ref.dtype)

    spec = pl.BlockSpec((tm, hd), lambda i: (i, 0))
    return pl.pallas_call(
        body,
        grid=(m // tm,),
        in_specs=[spec] * (2 * ng),
        out_specs=spec,
        out_shape=jax.ShapeDtypeStruct((m, hd), BF16),
        compiler_params=_params(("arbitrary",), 40),
    )(*outs, *lses)


def _dil_sample(qkv_s, caches, table, *, heads, head_dim):
    bd = qkv_s.shape[0]
    hd = heads * head_dim
    ng = len(DIL_GROUPS)
    nk = DIL_GROUPS[0][0] // DIL_GROUPS[0][1]
    views, onehots = [], []
    for (window, dil), c in zip(DIL_GROUPS, caches):
        assert c.shape[1] == window and window // dil == nk
        views.append(c.reshape(bd, window // dil, dil * 2 * hd))
        dist = (nk - np.arange(nk)) * dil
        oh = np.zeros((nk, LANES), np.float32)
        oh[np.arange(nk), _bucket_np(dist)] = 1.0
        onehots.append(oh)
    onehot = jnp.asarray(np.stack(onehots), BF16)
    tpad = jnp.zeros((LANES, LANES), F32).at[:N_BUCKETS, :heads].set(table.astype(F32))
    head_of = np.arange(hd) // head_dim
    sel_np = (head_of[:, None] == np.arange(LANES)[None, :]).astype(np.float32)
    sel = jnp.asarray(sel_np, BF16)
    sel_t = jnp.asarray(sel_np.T, BF16)
    scale = head_dim ** -0.5

    def body(tpad_ref, oh_ref, sel_ref, selt_ref, qkv_ref, c0_ref, c1_ref, c2_ref, o_ref, bias_ref):
        @pl.when(pl.program_id(0) == 0)
        def _():
            for g in range(ng):
                bias_ref[g] = _dot_sel_left(oh_ref[g], tpad_ref[...])

        head_lane = lax.broadcasted_iota(I32, (1, LANES), 1) < heads
        c_refs = (c0_ref, c1_ref, c2_ref)
        outs, lses = [], []
        for g in range(ng):
            q = qkv_ref[0, :, g * hd:(g + 1) * hd] * scale
            kn = qkv_ref[0, :, (ng + g) * hd:(ng + g + 1) * hd]
            vn = qkv_ref[0, :, (2 * ng + g) * hd:(2 * ng + g + 1) * hd]
            kc = c_refs[g][0, :, 0:hd]
            vc = c_refs[g][0, :, hd:2 * hd]
            s = _dot_sel_right(kc * q, sel_ref[...]) + bias_ref[g]
            s_new = _dot_sel_right(jnp.broadcast_to(kn * q, (8, hd)), sel_ref[...])[0:1] + tpad_ref[0:1, :]
            mx = jnp.maximum(jnp.max(s, axis=0, keepdims=True), s_new)
            e = jnp.exp(s - mx)
            e_new = jnp.exp(s_new - mx)
            den = jnp.sum(e, axis=0, keepdims=True) + e_new
            p = _dot_sel_right(e / den, selt_ref[...])
            p_new = _dot_sel_right(jnp.broadcast_to(e_new / den, (8, LANES)), selt_ref[...])[0:1]
            outs.append(jnp.sum(p * vc, axis=0, keepdims=True) + p_new * vn)
            lses.append(jnp.where(head_lane, mx + jnp.log(den), 0.0))
        mx = functools.reduce(jnp.maximum, lses)
        ws = [jnp.exp(l - mx) for l in lses]
        tot = functools.reduce(lambda a, b: a + b, ws)
        res = jnp.zeros((1, hd), F32)
        for g in range(ng):
            cg = _dot_sel_right(jnp.broadcast_to(ws[g] / tot, (8, LANES)), selt_ref[...])[0:1]
            res = res + cg * outs[g]
        o_ref[0] = res

    full = lambda shape: pl.BlockSpec(shape, lambda b: (0,) * len(shape))
    out = pl.pallas_call(
        body,
        grid=(bd,),
        in_specs=[
            full((LANES, LANES)), full((ng, nk, LANES)), full((hd, LANES)), full((LANES, hd)),
            pl.BlockSpec((1, 1, 3 * ng * hd), lambda b: (b, 0, 0)),
        ] + [pl.BlockSpec((1, nk, 2 * hd), lambda b: (b, 0, 0)) for _ in range(ng)],
        out_specs=pl.BlockSpec((1, 1, hd), lambda b: (b, 0, 0)),
        out_shape=jax.ShapeDtypeStruct((bd, 1, hd), F32),
        scratch_shapes=[pltpu.VMEM((ng, nk, LANES), F32)],
        compiler_params=_params(("arbitrary",), 32),
    )(tpad, onehot, sel, sel_t, qkv_s.reshape(bd, 1, -1), *views)
    return out.reshape(bd, hd)


def _block_rank_select(gates, valid, n_sel):
    sels = []
    nblk = len(gates)
    for k in range(nblk):
        rank = jnp.zeros(gates[k].shape, F32)
        for m in range(nblk):
            if m == k:
                continue
            beats = (gates[m] >= gates[k]) if m < k else (gates[m] > gates[k])
            if valid is not None:
                beats = beats & valid[m]
            rank = rank + beats.astype(F32)
        s = rank < n_sel
        if valid is not None:
            s = s & valid[k]
        sels.append(s)
    return sels


def _moba_prompt(qkv, table, *, batch, seq, heads, kv_heads):
    d = B_HEAD_DIM
    blk = MOBA_BLOCK
    nblk = seq // blk
    grp = heads // kv_heads
    n_sel = min(MOBA_TOPK, nblk)
    scale = d ** -0.5
    r = np.arange(blk)[:, None]
    c = np.arange(blk)[None, :]
    bkt = jnp.asarray(np.stack([_bucket_np(db * blk + r - c) for db in range(nblk)]))

    def body(table_ref, bkt_ref, q_ref, k_ref, v_ref, o_ref, bias_ref, m_ref, l_ref, acc_ref):
        h = pl.program_id(0)
        i = pl.program_id(2)

        @pl.when((pl.program_id(1) == 0) & (i == 0))
        def _():
            row = lax.broadcasted_iota(I32, (blk, blk), 0)
            col = lax.broadcasted_iota(I32, (blk, blk), 1)
            for db in range(nblk):
                bk = bkt_ref[db]
                t = lax.fori_loop(0, N_BUCKETS, lambda k, t: jnp.where(bk == k, table_ref[k, h], t),
                                  jnp.zeros((blk, blk), F32))
                if db == 0:
                    t = jnp.where(col > row, NEG_INF, t)
                bias_ref[db] = t

        q = q_ref[...]
        qb = (q * scale).astype(BF16)
        row16 = lax.broadcasted_iota(I32, (16, d), 0)
        kmean = jnp.zeros((16, d), F32)
        for n in range(nblk):
            kmean = jnp.where(row16 == n, jnp.mean(k_ref[n * blk:(n + 1) * blk, :], axis=0, keepdims=True), kmean)
        q1, q2, q3 = _split3(q)
        k1, k2, k3 = _split3(kmean)
        gate_t = (_dot_nt(k1, q1) + _dot_nt(k1, q2) + _dot_nt(k2, q1)
                  + _dot_nt(k2, q2) + _dot_nt(k1, q3) + _dot_nt(k3, q1))
        gates = [gate_t[n:n + 1, :] for n in range(nblk)]
        valid = [jnp.full((1, blk), n, I32) < i for n in range(nblk)]
        sels = _block_rank_select(gates, valid, n_sel)
        row128 = lax.broadcasted_iota(I32, (LANES, blk), 0)
        sel_t = jnp.zeros((LANES, blk), F32)
        for n in range(nblk):
            sel_t = jnp.where((row128 == n) & sels[n], 1.0, sel_t)
        sel = sel_t.T

        m_ref[...] = jnp.full(m_ref.shape, NEG_INF, F32)
        l_ref[...] = jnp.zeros(l_ref.shape, F32)
        acc_ref[...] = jnp.zeros(acc_ref.shape, F32)

        def update(s, vb):
            m_old = m_ref[...]
            m_new = jnp.maximum(m_old, jnp.max(s, axis=-1, keepdims=True))
            alpha = jnp.exp(m_old - m_new)
            e = jnp.exp(s - m_new)
            l_ref[...] = alpha * l_ref[...] + jnp.sum(e, axis=-1, keepdims=True)
            acc_ref[...] = alpha * acc_ref[...] + _dot(e.astype(BF16), vb)
            m_ref[...] = m_new

        for n in range(nblk - 1):
            @pl.when(n < i)
            def _(n=n):
                kb = k_ref[n * blk:(n + 1) * blk, :].astype(BF16)
                vb = v_ref[n * blk:(n + 1) * blk, :].astype(BF16)
                s = _dot_nt(qb, kb) + bias_ref[i - n]
                update(jnp.where(sel[:, n:n + 1] > 0.5, s, NEG_INF), vb)

        own = pl.multiple_of(i * blk, blk)
        kb = k_ref[pl.ds(own, blk), :].astype(BF16)
        vb = v_ref[pl.ds(own, blk), :].astype(BF16)
        update(_dot_nt(qb, kb) + bias_ref[0], vb)
        o_ref[...] = (acc_ref[...] / l_ref[...]).astype(o_ref.dtype)

    return pl.pallas_call(
        body,
        grid=(heads, batch, nblk),
        in_specs=[
            pl.BlockSpec(memory_space=pltpu.SMEM),
            pl.BlockSpec((nblk, blk, blk), lambda h, b, i: (0, 0, 0)),
            pl.BlockSpec((blk, d), lambda h, b, i: (b * nblk + i, h)),
            pl.BlockSpec((seq, d), lambda h, b, i: (b, heads + h // grp)),
            pl.BlockSpec((seq, d), lambda h, b, i: (b, heads + kv_heads + h // grp)),
        ],
        out_specs=pl.BlockSpec((blk, d), lambda h, b, i: (b * nblk + i, h)),
        out_shape=jax.ShapeDtypeStruct((batch * seq, heads * d), BF16),
        scratch_shapes=[
            pltpu.VMEM((nblk, blk, blk), F32),
            pltpu.VMEM((blk, 1), F32), pltpu.VMEM((blk, 1), F32), pltpu.VMEM((blk, d), F32),
        ],
        compiler_params=_params(("arbitrary", "arbitrary", "arbitrary"), 32),
    )(table, bkt, qkv, qkv, qkv)


def _moba_sample(q, k_new, v_new, cache, page_table, table, *, heads, kv_heads):
    bd = q.shape[0]
    d = B_HEAD_DIM
    blk = MOBA_BLOCK
    n_pages = page_table.shape[1]
    past = n_pages * PAGE_SIZE
    nblk = past // blk
    assert blk == 2 * PAGE_SIZE and past % blk == 0
    grp = heads // kv_heads
    n_sel = min(MOBA_TOPK, nblk + 1)
    scale = d ** -0.5
    kvw = kv_heads * d
    pages = cache.reshape(cache.shape[0], PAGE_SIZE, 2 * kvw)
    oh = np.zeros((LANES, past), np.float32)
    oh[_bucket_np(past - np.arange(past)), np.arange(past)] = 1.0
    onehot_t = jnp.asarray(oh, BF16)
    t_t = jnp.zeros((heads, LANES), F32).at[:, :N_BUCKETS].set(table.astype(F32).T)

    def body(pt_ref, tt_ref, oh_ref, q_ref, kn_ref, vn_ref, p0_ref, p1_ref, o_ref,
             bias_ref, m_ref, l_ref, acc_ref, gate_ref):
        b = pl.program_id(0)
        n = pl.program_id(1)

        @pl.when((b == 0) & (n == 0))
        def _():
            t1, t2, t3 = _split3(tt_ref[...])
            full = _dot(t1, oh_ref[...]) + _dot(t2, oh_ref[...]) + _dot(t3, oh_ref[...])
            for k in range(nblk):
                bias_ref[k] = full[:, k * blk:(k + 1) * blk]

        qv = q_ref[0]
        qb = (qv * scale).astype(BF16)
        row_grp = lax.broadcasted_iota(I32, (heads, 1), 0) // grp
        s = jnp.zeros((heads, blk), F32)
        gate = jnp.zeros((heads, 1), F32)
        for g in range(kv_heads):
            kg = jnp.concatenate([p0_ref[0, :, g * d:(g + 1) * d], p1_ref[0, :, g * d:(g + 1) * d]], axis=0)
            sg = _dot_nt(qb, kg.astype(BF16))
            gg = jnp.sum(qv * jnp.mean(kg, axis=0, keepdims=True), axis=1, keepdims=True)
            s = jnp.where(row_grp == g, sg, s)
            gate = jnp.where(row_grp == g, gg, gate)
        s = s + bias_ref[n]
        mx = jnp.max(s, axis=1, keepdims=True)
        e = jnp.exp(s - mx)
        den = jnp.sum(e, axis=1, keepdims=True)
        eb = e.astype(BF16)
        acc = jnp.zeros((heads, d), F32)
        for g in range(kv_heads):
            vg = jnp.concatenate([p0_ref[0, :, kvw + g * d:kvw + (g + 1) * d],
                                  p1_ref[0, :, kvw + g * d:kvw + (g + 1) * d]], axis=0)
            acc = jnp.where(row_grp == g, _dot(eb, vg.astype(BF16)), acc)
        m_ref[n] = jnp.broadcast_to(mx, (heads, d))
        l_ref[n] = jnp.broadcast_to(den, (heads, d))
        gate_ref[n] = jnp.broadcast_to(gate, (heads, d))
        acc_ref[n] = acc

        @pl.when(n == nblk - 1)
        def _():
            gates = [gate_ref[k] for k in range(nblk)]
            sels = _block_rank_select(gates, None, n_sel)
            s_own = jnp.sum(qv * kn_ref[0], axis=1, keepdims=True) * scale + tt_ref[:, 0:1]
            big = jnp.broadcast_to(s_own, (heads, d))
            for k in range(nblk):
                big = jnp.maximum(big, jnp.where(sels[k], m_ref[k], NEG_INF))
            w_own = jnp.exp(s_own - big)
            tot = w_own
            out = w_own * vn_ref[0]
            for k in range(nblk):
                wk = jnp.where(sels[k], jnp.exp(m_ref[k] - big), 0.0)
                tot = tot + wk * l_ref[k]
                out = out + wk * acc_ref[k]
            o_ref[0] = out / tot

    def page_spec(which):
        return pl.BlockSpec((1, PAGE_SIZE, 2 * kvw), lambda b, n, pt: (pt[b * n_pages + 2 * n + which], 0, 0))

    row3 = pl.BlockSpec((1, heads, d), lambda b, n, pt: (b, 0, 0))
    stat = pltpu.VMEM((nblk, heads, d), F32)
    return pl.pallas_call(
        body,
        grid_spec=pltpu.PrefetchScalarGridSpec(
            num_scalar_prefetch=1,
            grid=(bd, nblk),
            in_specs=[
                pl.BlockSpec((heads, LANES), lambda b, n, pt: (0, 0)),
                pl.BlockSpec((LANES, past), lambda b, n, pt: (0, 0)),
                row3, row3, row3, page_spec(0), page_spec(1),
            ],
            out_specs=row3,
            scratch_shapes=[pltpu.VMEM((nblk, heads, blk), F32), stat, stat, stat, stat],
        ),
        out_shape=jax.ShapeDtypeStruct((bd, heads, d), F32),
        compiler_params=_params(("arbitrary", "arbitrary"), 32),
    )(page_table.reshape(-1).astype(I32), t_t, onehot_t, q, k_new, v_new, pages, pages)


def _router(x, gamma, w_router, *, tm):
    m, d = x.shape
    ne = w_router.shape[1]
    w_pad = jnp.zeros((d, LANES), F32).at[:, :ne].set(w_router.astype(F32))

    def body(x_ref, g_ref, w_ref, h_ref, idx_ref, gate_ref):
        xv = x_ref[...]
        h = xv * lax.rsqrt(jnp.mean(xv * xv, axis=-1, keepdims=True) + RMS_EPS) * g_ref[...]
        h_ref[...] = h
        h1, h2, h3 = _split3(h)
        w1, w2, w3 = _split3(w_ref[...])
        lg = (_dot(h1, w1) + _dot(h1, w2) + _dot(h2, w1) + _dot(h2, w2) + _dot(h1, w3) + _dot(h3, w1))
        lane = lax.broadcasted_iota(I32, lg.shape, 1)
        lane_f = lane.astype(F32)
        lg = jnp.where(lane < ne, lg, -jnp.inf)
        v1 = jnp.max(lg, axis=-1, keepdims=True)
        i1 = jnp.min(jnp.where(lg == v1, lane_f, float(LANES)), axis=-1, keepdims=True)
        lg2 = jnp.where(lane_f == i1, -jnp.inf, lg)
        v2 = jnp.max(lg2, axis=-1, keepdims=True)
        i2 = jnp.min(jnp.where(lg2 == v2, lane_f, float(LANES)), axis=-1, keepdims=True)
        e2 = jnp.exp(v2 - v1)
        g1 = 1.0 / (1.0 + e2)
        g2 = e2 / (1.0 + e2)
        idx_ref[...] = jnp.where(lane == 0, i1, jnp.where(lane == 1, i2, 0.0)).astype(I32)
        gate_ref[...] = jnp.where(lane == 0, g1, jnp.where(lane == 1, g2, 0.0))

    return pl.pallas_call(
        body,
        grid=(m // tm,),
        in_specs=[pl.BlockSpec((tm, d), lambda i: (i, 0)), pl.BlockSpec((1, d), lambda i: (0, 0)),
                  pl.BlockSpec((d, LANES), lambda i: (0, 0))],
        out_specs=[pl.BlockSpec((tm, d), lambda i: (i, 0)), pl.BlockSpec((tm, LANES), lambda i: (i, 0)),
                   pl.BlockSpec((tm, LANES), lambda i: (i, 0))],
        out_shape=[jax.ShapeDtypeStruct((m, d), F32), jax.ShapeDtypeStruct((m, LANES), I32),
                   jax.ShapeDtypeStruct((m, LANES), F32)],
        compiler_params=_params(("arbitrary",), 40),
    )(x, gamma.reshape(1, d).astype(F32), w_pad)


def _gather_rows(src, idx, n_out, *, rows_per_step):
    slab = src.shape[1:]
    assert n_out % rows_per_step == 0

    def body(idx_ref, src_ref, dst_ref, sem):
        base = pl.program_id(0) * rows_per_step

        def copy(r):
            return pltpu.make_async_copy(src_ref.at[idx_ref[base + r]], dst_ref.at[base + r], sem.at[0])

        def start(r, carry):
            copy(r).start()
            return carry

        def wait(r, carry):
            copy(r).wait()
            return carry

        lax.fori_loop(0, rows_per_step, start, 0)
        lax.fori_loop(0, rows_per_step, wait, 0)

    return pl.pallas_call(
        body,
        grid_spec=pltpu.PrefetchScalarGridSpec(
            num_scalar_prefetch=1,
            grid=(n_out // rows_per_step,),
            in_specs=[pl.BlockSpec(memory_space=pl.ANY)],
            out_specs=pl.BlockSpec(memory_space=pl.ANY),
            scratch_shapes=[pltpu.SemaphoreType.DMA((1,))],
        ),
        out_shape=jax.ShapeDtypeStruct((n_out,) + slab, src.dtype),
        compiler_params=pltpu.CompilerParams(dimension_semantics=("arbitrary",)),
    )(idx, src)


def _combine(x3, ys3, dest, gamma, *, tc):
    n, s, _ = x3.shape
    d = s * LANES

    def body(dest_ref, x_ref, ys_ref, g_ref, o_ref, buf, sems):
        base = pl.program_id(0) * tc

        def copy(r, k):
            return pltpu.make_async_copy(ys_ref.at[dest_ref[2 * (base + r) + k]], buf.at[k, r], sems.at[k])

        def start(r, carry):
            copy(r, 0).start()
            copy(r, 1).start()
            return carry

        def wait(r, carry):
            copy(r, 0).wait()
            copy(r, 1).wait()
            return carry

        lax.fori_loop(0, tc, start, 0)
        lax.fori_loop(0, tc, wait, 0)
        xv = x_ref[...] + buf[0] + buf[1]
        ms = jnp.sum(jnp.sum(xv * xv, axis=2, keepdims=True), axis=1, keepdims=True) * (1.0 / d)
        o_ref[...] = xv * lax.rsqrt(ms + RMS_EPS) * g_ref[...]

    return pl.pallas_call(
        body,
        grid_spec=pltpu.PrefetchScalarGridSpec(
            num_scalar_prefetch=1,
            grid=(n // tc,),
            in_specs=[pl.BlockSpec((tc, s, LANES), lambda i, dr: (i, 0, 0)),
                      pl.BlockSpec(memory_space=pl.ANY),
                      pl.BlockSpec((1, s, LANES), lambda i, dr: (0, 0, 0))],
            out_specs=pl.BlockSpec((tc, s, LANES), lambda i, dr: (i, 0, 0)),
            scratch_shapes=[pltpu.VMEM((2, tc, s, LANES), F32), pltpu.SemaphoreType.DMA((2,))],
        ),
        out_shape=jax.ShapeDtypeStruct((n, s, LANES), F32),
        compiler_params=_params(("arbitrary",), 6 * tc * d * 4 // MIB + 8),
    )(dest, x3, ys3, gamma.reshape(1, s, LANES).astype(F32))


def _moe(x, gamma_ffn, w_router, wg, wu, wd, gamma_final, *, tm, tf, router_tile, combine_tile):
    n, d = x.shape
    ne = wg.shape[0]
    h, idx, gates = _router(x, gamma_ffn, w_router, tm=router_tile)
    e_flat = idx[:, :TOP_K].reshape(-1)
    onehot = (e_flat[:, None] == jnp.arange(ne, dtype=I32)[None, :]).astype(I32)
    csum = jnp.cumsum(onehot, axis=0)
    rank = jnp.sum((csum - onehot) * onehot, axis=1)
    counts = csum[-1]
    padded = (counts + tm - 1) // tm * tm
    pad_end = jnp.cumsum(padded)
    dest = ((pad_end - padded)[e_flat] + rank).astype(I32)
    n_tiles = -(-(n * TOP_K + ne * (tm - 1)) // tm)
    n_rows = n_tiles * tm
    tile_expert = jnp.minimum(
        jnp.searchsorted(pad_end, jnp.arange(n_tiles, dtype=I32) * tm, side='right'), ne - 1).astype(I32)
    n_used = (pad_end[-1] // tm).astype(I32).reshape(1)
    row_tok = jnp.zeros((n_rows,), I32).at[dest].set(jnp.arange(n * TOP_K, dtype=I32) // TOP_K)
    row_gate = jnp.zeros((n_rows,), F32).at[dest].set(gates[:, :TOP_K].reshape(-1))

    s = d // LANES
    xs = _gather_rows(h.reshape(n, s, LANES), row_tok, n_rows, rows_per_step=tm).reshape(n_rows, d)
    ys = _ffn(xs, wg, wu, wd, tile_expert, n_used, row_gate=row_gate, add_res=False, tm=tm, tf=tf)
    y = _combine(x.reshape(n, s, LANES), ys.reshape(n_rows, s, LANES), dest, gamma_final, tc=combine_tile)
    return y.reshape(n, d)


def kernel(x_prompt, x_sample, cache_win0_kv, cache_win1_kv, cache_win2_kv, cache_moba_kv, page_table, rel_bias, norm_mix, norm_ffn, norm_final, w_qkv_dil, w_o_dil, w_qkv_moba, w_o_moba, w_gate_dense, w_up_dense, w_down_dense, w_router, w_gate_moe, w_up_moe, w_down_moe):
    bp, sp, d = x_prompt.shape
    bd = x_sample.shape[0]
    n_p = bp * sp
    n = n_p + bd
    assert x_sample.shape[1] == 1 and n % ROW_TILE == 0
    table = rel_bias.astype(F32)
    x0 = jnp.concatenate([x_prompt.reshape(n_p, d), x_sample.reshape(bd, d)], axis=0)

    hd = A_SLOTS * A_HEAD_DIM
    ng = len(DIL_GROUPS)
    qkv_a = _matmul(x0, w_qkv_dil[0].astype(BF16), gamma=norm_mix[0], tm=ROW_TILE, tn=1024)
    outs, lses = [], []
    for g, (window, dil) in enumerate(DIL_GROUPS):
        o, lse = _dil_prompt_group(qkv_a, table, g, window, dil, batch=bp, seq=sp, heads=A_SLOTS, head_dim=A_HEAD_DIM)
        outs.append(o)
        lses.append(lse)
    o_p = _merge_groups(outs, lses, tm=256)
    caches = [c[0] for c in (cache_win0_kv, cache_win1_kv, cache_win2_kv)]
    o_s = _dil_sample(qkv_a[n_p:], caches, table, heads=A_SLOTS, head_dim=A_HEAD_DIM)
    o_a = jnp.concatenate([o_p, o_s.astype(BF16)], axis=0)
    x1 = _matmul(o_a, w_o_dil[0].astype(BF16), residual=x0, tm=ROW_TILE, tn=1024)
    dense_tiles = n // ROW_TILE
    x2 = _ffn(x1, w_gate_dense.astype(BF16), w_up_dense.astype(BF16), w_down_dense.astype(BF16),
              jnp.zeros((dense_tiles,), I32), jnp.full((1,), dense_tiles, I32),
              gamma=norm_ffn[0], add_res=True, tm=ROW_TILE, tf=FF_TILE)

    qd, kd = B_HEADS * B_HEAD_DIM, B_KV_HEADS * B_HEAD_DIM
    qkv_b = _matmul(x2, w_qkv_moba[0].astype(BF16), gamma=norm_mix[1], tm=ROW_TILE, tn=1024)
    m_p = _moba_prompt(qkv_b, table, batch=bp, seq=sp, heads=B_HEADS, kv_heads=B_KV_HEADS)
    qkv_bs = qkv_b[n_p:]
    grp = B_HEADS // B_KV_HEADS
    q_s = qkv_bs[:, :qd].reshape(bd, B_HEADS, B_HEAD_DIM)
    k_s = qkv_bs[:, qd:qd + kd].reshape(bd, B_KV_HEADS, B_HEAD_DIM)
    v_s = qkv_bs[:, qd + kd:].reshape(bd, B_KV_HEADS, B_HEAD_DIM)
    m_s = _moba_sample(q_s, jnp.repeat(k_s, grp, axis=1), jnp.repeat(v_s, grp, axis=1), cache_moba_kv[0],
                       page_table, table, heads=B_HEADS, kv_heads=B_KV_HEADS)
    o_b = jnp.concatenate([m_p, m_s.reshape(bd, qd).astype(BF16)], axis=0)
    x3 = _matmul(o_b, w_o_moba[0].astype(BF16), residual=x2, tm=ROW_TILE, tn=1024)
    y = _moe(x3, norm_ffn[1], w_router[0], w_gate_moe[0].astype(BF16), w_up_moe[0].astype(BF16),
             w_down_moe[0].astype(BF16), norm_final,
             tm=MOE_ROW_TILE, tf=FF_TILE, router_tile=ROW_TILE, combine_tile=COMBINE_TILE)

    y_prompt = y[:n_p].reshape(bp, sp, d)
    y_sample = y[n_p:].reshape(bd, 1, d)
    qa_p = qkv_a[:n_p].reshape(bp, sp, 3, ng, A_SLOTS, A_HEAD_DIM)
    qa_s = qkv_a[n_p:].reshape(bd, 1, 3, ng, A_SLOTS, A_HEAD_DIM)
    win_p, win_s = [], []
    for g, (window, _) in enumerate(DIL_GROUPS):
        keep = min(window, sp)
        win_p.append(jnp.stack([qa_p[:, sp - keep:, 1, g], qa_p[:, sp - keep:, 2, g]], axis=2)[None])
        win_s.append(jnp.stack([qa_s[:, :, 1, g], qa_s[:, :, 2, g]], axis=2)[None])
    kv_p = qkv_b[:n_p, qd:].reshape(bp, sp // PAGE_SIZE, PAGE_SIZE, 2, B_KV_HEADS, B_HEAD_DIM)[None]
    kv_s = qkv_bs[:, qd:].reshape(bd, 1, 2, B_KV_HEADS, B_HEAD_DIM)[None]
    return (y_prompt, y_sample, win_p[0], win_p[1], win_p[2], win_s[0], win_s[1], win_s[2], kv_p, kv_s)
```

```python
import functools
import math

import numpy as np
import jax
import jax.numpy as jnp
from jax import lax
from jax.experimental import pallas as pl
from jax.experimental.pallas import tpu as pltpu

F32 = jnp.float32
BF16 = jnp.bfloat16
I32 = jnp.int32

RMS_EPS = 1e-6
NEG_INF = -1e30
MIB = 1024 * 1024

N_BUCKETS = 32
MAX_DISTANCE = 2048
DIL_GROUPS = ((128, 1), (512, 4), (2048, 16))
A_SLOTS = 16
A_HEAD_DIM = 64
B_HEADS = 16
B_KV_HEADS = 4
B_HEAD_DIM = 128
MOBA_BLOCK = 256
MOBA_TOPK = 3
PAGE_SIZE = 128
N_EXPERTS = 8
TOP_K = 2

LANES = 128

ROW_TILE = 688
MOE_ROW_TILE = 512
FF_TILE = 512
COMBINE_TILE = 384


def _params(sem, vmem_mib):
    return pltpu.CompilerParams(dimension_semantics=sem, vmem_limit_bytes=vmem_mib * MIB)


def _bucket_np(dist):
    max_exact = N_BUCKETS // 2
    n = np.maximum(np.asarray(dist, np.int64), 0)
    nf = np.maximum(n, 1).astype(np.float32)
    scale = np.float32((N_BUCKETS - max_exact) / math.log(MAX_DISTANCE / max_exact))
    large = max_exact + (np.log(nf / np.float32(max_exact)) * scale).astype(np.int32)
    return np.where(n < max_exact, n, np.minimum(large, N_BUCKETS - 1)).astype(np.int32)


def _split3(t):
    t1 = t.astype(BF16)
    r1 = t - t1.astype(F32)
    t2 = r1.astype(BF16)
    r2 = r1 - t2.astype(F32)
    return t1, t2, r2.astype(BF16)


def _aligned(x, m):
    return x if isinstance(x, int) else pl.multiple_of(x, m)


def _dot(a, b):
    return jnp.dot(a, b, preferred_element_type=F32)


def _dot_nt(a, b):
    return lax.dot_general(a, b, (((1,), (1,)), ((), ())), preferred_element_type=F32)


def _matmul(x, w, gamma=None, residual=None, *, tm, tn):
    m, k = x.shape
    n = w.shape[1]
    prenorm = gamma is not None
    stage = prenorm or x.dtype != BF16
    has_res = residual is not None

    def body(*refs):
        it = iter(refs)
        x_ref = next(it)
        g_ref = next(it) if prenorm else None
        w_ref = next(it)
        r_ref = next(it) if has_res else None
        o_ref = next(it)
        hb_ref = next(it) if stage else None
        if stage:
            @pl.when(pl.program_id(1) == 0)
            def _():
                xv = x_ref[...].astype(F32)
                if prenorm:
                    xv = xv * lax.rsqrt(jnp.mean(xv * xv, axis=-1, keepdims=True) + RMS_EPS) * g_ref[...]
                hb_ref[...] = xv.astype(BF16)
            a = hb_ref[...]
        else:
            a = x_ref[...]
        acc = _dot(a, w_ref[...])
        if has_res:
            acc = acc + r_ref[...]
        o_ref[...] = acc

    in_specs = [pl.BlockSpec((tm, k), lambda i, j: (i, 0))]
    args = [x]
    if prenorm:
        in_specs.append(pl.BlockSpec((1, k), lambda i, j: (0, 0)))
        args.append(gamma.reshape(1, k).astype(F32))
    in_specs.append(pl.BlockSpec((k, tn), lambda i, j: (0, j)))
    args.append(w)
    if has_res:
        in_specs.append(pl.BlockSpec((tm, tn), lambda i, j: (i, j)))
        args.append(residual)
    scratch = [pltpu.VMEM((tm, k), BF16)] if stage else []
    vmem = 2 * (tm * k * x.dtype.itemsize + k * tn * 2 + tm * tn * 4 * (2 if has_res else 1)) + tm * k * 2
    return pl.pallas_call(
        body,
        grid=(m // tm, n // tn),
        in_specs=in_specs,
        out_specs=pl.BlockSpec((tm, tn), lambda i, j: (i, j)),
        out_shape=jax.ShapeDtypeStruct((m, n), F32),
        scratch_shapes=scratch,
        compiler_params=_params(("arbitrary", "arbitrary"), vmem // MIB + 8),
    )(*args)


def _ffn(x, wg, wu, wd, tile_expert, n_used, gamma=None, row_gate=None, row_tok=None, *, add_res, tm, tf,
         n_rows=None):
    gather = row_tok is not None
    d = wg.shape[1]
    s = d // LANES
    m = n_rows if gather else x.shape[0]
    ff = wg.shape[2]
    nj = ff // tf
    n_tiles = m // tm
    prenorm = gamma is not None
    use_gate = row_gate is not None
    n_pf = 3 if gather else 2
    assert not (gather and (add_res or prenorm))

    def body(*refs):
        te_ref, nu_ref = refs[0], refs[1]
        tok_ref = refs[2] if gather else None
        it = iter(refs[n_pf:])
        x_ref = next(it)
        g_ref = next(it) if prenorm else None
        wg_ref, wu_ref, wd_ref = next(it), next(it), next(it)
        rg_ref = next(it) if use_gate else None
        o_ref = next(it)
        hb_ref = next(it)
        xg_ref, sems = (next(it), next(it)) if gather else (None, None)
        i = pl.program_id(0)
        j = pl.program_id(1)
        active = i < nu_ref[0]

        def row_copy(tile, slot, r):
            tok = tok_ref[tile * tm + r]
            return pltpu.make_async_copy(
                x_ref.at[pl.ds(_aligned(tok * s, s), s), :],
                xg_ref.at[slot, pl.ds(_aligned(r * s, s), s), :],
                sems.at[slot])

        def issue(tile, slot):
            def one(r, carry):
                row_copy(tile, slot, r).start()
                return carry
            lax.fori_loop(0, tm, one, 0)

        def drain(tile, slot):
            def one(r, carry):
                row_copy(tile, slot, r).wait()
                return carry
            lax.fori_loop(0, tm, one, 0)

        @pl.when(jnp.logical_and(active, j == 0))
        def _():
            if gather:
                slot = i % 2

                @pl.when(i == 0)
                def _():
                    issue(0, 0)

                drain(i, slot)
                for c in range(s):
                    hb_ref[:, c * LANES:(c + 1) * LANES] = xg_ref[slot, pl.ds(c, tm, stride=s), :].astype(BF16)

                @pl.when(i + 1 < nu_ref[0])
                def _():
                    issue(i + 1, 1 - slot)
            else:
                xv = x_ref[...]
                if prenorm:
                    xv = xv * lax.rsqrt(jnp.mean(xv * xv, axis=-1, keepdims=True) + RMS_EPS) * g_ref[...]
                hb_ref[...] = xv.astype(BF16)

        @pl.when(j == 0)
        def _():
            o_ref[...] = jnp.zeros_like(o_ref)

        @pl.when(active)
        def _():
            h = hb_ref[...]
            gt = _dot(h, wg_ref[0])
            up = _dot(h, wu_ref[0])
            a = (gt / (1.0 + jnp.exp(-gt)) * up).astype(BF16)
            o_ref[...] += _dot(a, wd_ref[0])

        @pl.when(jnp.logical_and(active, j == nj - 1))
        def _():
            r = o_ref[...]
            if use_gate:
                r = r * rg_ref[...]
            if add_res:
                r = r + x_ref[...]
            o_ref[...] = r

    def last_used(i, pf):
        return jnp.minimum(i, pf[1][0] - 1)

    def col_of(i, j, pf):
        return jnp.where(i < pf[1][0], j, nj - 1)

    if gather:
        in_specs = [pl.BlockSpec(memory_space=pl.ANY)]
    else:
        in_specs = [pl.BlockSpec((tm, d), lambda i, j, *pf: (last_used(i, pf), 0))]
    args = [x]
    if prenorm:
        in_specs.append(pl.BlockSpec((1, d), lambda i, j, *pf: (0, 0)))
        args.append(gamma.reshape(1, d).astype(F32))
    in_specs += [
        pl.BlockSpec((1, d, tf), lambda i, j, *pf: (pf[0][last_used(i, pf)], 0, col_of(i, j, pf))),
        pl.BlockSpec((1, d, tf), lambda i, j, *pf: (pf[0][last_used(i, pf)], 0, col_of(i, j, pf))),
        pl.BlockSpec((1, tf, d), lambda i, j, *pf: (pf[0][last_used(i, pf)], col_of(i, j, pf), 0)),
    ]
    args += [wg, wu, wd]
    if use_gate:
        in_specs.append(pl.BlockSpec((tm, 1), lambda i, j, *pf: (last_used(i, pf), 0)))
        args.append(row_gate.reshape(m, 1))
    scratch = [pltpu.VMEM((tm, d), BF16)]
    vmem = 2 * (3 * d * tf * 2 + tm * d * 4 + (tm * LANES * 4 if use_gate else 0)) + tm * d * 2 + 2 * tm * d * 4
    if gather:
        scratch += [pltpu.VMEM((2, tm * s, LANES), F32), pltpu.SemaphoreType.DMA((2,))]
    prefetch = (tile_expert, n_used) + ((row_tok,) if gather else ())
    return pl.pallas_call(
        body,
        grid_spec=pltpu.PrefetchScalarGridSpec(
            num_scalar_prefetch=n_pf,
            grid=(n_tiles, nj),
            in_specs=in_specs,
            out_specs=pl.BlockSpec((tm, d), lambda i, j, *pf: (i, 0)),
            scratch_shapes=scratch,
        ),
        out_shape=jax.ShapeDtypeStruct((m, d), F32),
        compiler_params=_params(("arbitrary", "arbitrary"), vmem // MIB + 10),
    )(*prefetch, *args)


def _dil_bucket_maps(w, dil):
    i = np.arange(w)[:, None]
    j = np.arange(2 * w)[None, :]
    delta = i + w - j
    valid = (delta >= 0) & (delta <= w)
    bk = np.where(valid, _bucket_np(delta * dil), -1).astype(np.int32)
    first = np.where(j >= w, bk, -1).astype(np.int32)
    return np.stack([first, bk], axis=0)


def _dil_prompt(qkv, table, *, batch, seq, heads, head_dim):
    hd = heads * head_dim
    ng = len(DIL_GROUPS)
    nhp = hd // LANES
    w = DIL_GROUPS[0][0] // DIL_GROUPS[0][1]
    assert all(win // dil == w and seq % (dil * w) == 0 for win, dil in DIL_GROUPS) and 2 * head_dim == LANES
    n_iter = seq // w
    scale = head_dim ** -0.5
    bkt = jnp.asarray(np.stack([_dil_bucket_maps(w, dil) for _, dil in DIL_GROUPS]))
    merge_rows = 256

    def body(table_ref, bkt_ref, *refs):
        q_refs, k_refs, v_refs = refs[0:ng], refs[ng:2 * ng], refs[2 * ng:3 * ng]
        o_ref, bias_ref, og_ref, lg_ref = refs[3 * ng:]
        hp = pl.program_id(0)

        @pl.when(pl.program_id(1) == 0)
        def _():
            for g in range(ng):
                for v in range(2):
                    bk = bkt_ref[g, v]
                    for half in range(2):
                        t = lax.fori_loop(
                            0, N_BUCKETS, lambda k, t: jnp.where(bk == k, table_ref[k, 2 * hp + half], t),
                            jnp.zeros(bk.shape, F32))
                        bias_ref[g, v, half] = jnp.where(bk < 0, NEG_INF, t)

        lane = lax.broadcasted_iota(I32, (w, LANES), 1)
        low = lane < head_dim
        for g, (_, dil) in enumerate(DIL_GROUPS):
            nb = seq // (dil * w)

            def one_block(idx, carry, g=g, dil=dil, nb=nb):
                r = idx // nb
                n = idx - r * nb
                if dil == 1:
                    rows_c = pl.ds(_aligned(n * w, w), w)
                    rows_p = pl.ds(_aligned(jnp.maximum(n - 1, 0) * w, w), w)
                else:
                    rows_c = pl.ds(n * (w * dil) + r, w, stride=dil)
                    rows_p = pl.ds(jnp.maximum(n - 1, 0) * (w * dil) + r, w, stride=dil)
                q2 = q_refs[g][rows_c, :] * scale
                k2 = jnp.concatenate([k_refs[g][rows_p, :], k_refs[g][rows_c, :]], axis=0).astype(BF16)
                v2 = jnp.concatenate([v_refs[g][rows_p, :], v_refs[g][rows_c, :]], axis=0).astype(BF16)
                variant = jnp.minimum(n, 1)
                outs, lses = [], []
                for half in range(2):
                    keep = low if half == 0 else jnp.logical_not(low)
                    qm = jnp.where(keep, q2, 0.0).astype(BF16)
                    sc = _dot_nt(qm, k2) + bias_ref[g, variant, half]
                    mx = jnp.max(sc, axis=-1, keepdims=True)
                    e = jnp.exp(sc - mx)
                    den = jnp.sum(e, axis=-1, keepdims=True)
                    outs.append(_dot(e.astype(BF16), v2) / den)
                    lses.append(mx + jnp.log(den))
                og_ref[g, rows_c, :] = jnp.where(low, outs[0], outs[1])
                lg_ref[g, rows_c, :] = jnp.where(low, lses[0], lses[1])
                return carry

            lax.fori_loop(0, n_iter, one_block, 0)

        def merge(c, carry):
            rows = pl.ds(_aligned(c * merge_rows, merge_rows), merge_rows)
            ls = [lg_ref[g, rows, :] for g in range(ng)]
            mx = functools.reduce(jnp.maximum, ls)
            ws = [jnp.exp(l - mx) for l in ls]
            num = functools.reduce(lambda a, b: a + b, [ws[g] * og_ref[g, rows, :] for g in range(ng)])
            o_ref[rows, :] = (num / functools.reduce(lambda a, b: a + b, ws)).astype(o_ref.dtype)
            return carry

        lax.fori_loop(0, seq // merge_rows, merge, 0)

    def blk(which, g):
        return pl.BlockSpec((seq, LANES), lambda hp, b: (b, (which * ng + g) * nhp + hp))

    in_specs = [pl.BlockSpec(memory_space=pltpu.SMEM),
                pl.BlockSpec((ng, 2, w, 2 * w), lambda hp, b: (0, 0, 0, 0))]
    in_specs += [blk(which, g) for which in range(3) for g in range(ng)]
    return pl.pallas_call(
        body,
        grid=(nhp, batch),
        in_specs=in_specs,
        out_specs=pl.BlockSpec((seq, LANES), lambda hp, b: (b, hp)),
        out_shape=jax.ShapeDtypeStruct((batch * seq, hd), BF16),
        scratch_shapes=[pltpu.VMEM((ng, 2, 2, w, 2 * w), F32), pltpu.VMEM((ng, seq, LANES), F32),
                        pltpu.VMEM((ng, seq, LANES), F32)],
        compiler_params=_params(("arbitrary", "arbitrary"), 48),
    )(table, bkt, *([qkv] * (3 * ng)))


def _dil_sample(qkv_s, caches, table, *, heads, head_dim):
    bd = qkv_s.shape[0]
    ng = len(DIL_GROUPS)
    nk = DIL_GROUPS[0][0] // DIL_GROUPS[0][1]
    views, dists = [], []
    for (window, dil), c in zip(DIL_GROUPS, caches):
        assert c.shape[1] == window and window // dil == nk
        views.append(c.reshape(bd, nk, dil, 2, heads, head_dim))
        dists.append(_bucket_np((nk - np.arange(nk)) * dil))
    bias = table[np.stack(dists)][..., None]
    bias0 = table[0][None, :, None]
    q4 = qkv_s.reshape(bd, 3 * ng, heads, head_dim)
    scale = head_dim ** -0.5

    def body(bias_ref, b0_ref, q_ref, c0_ref, c1_ref, c2_ref, o_ref):
        c_refs = (c0_ref, c1_ref, c2_ref)
        outs, lses = [], []
        for g in range(ng):
            q = q_ref[0, g] * scale
            kn, vn = q_ref[0, ng + g], q_ref[0, 2 * ng + g]
            kc, vc = c_refs[g][:, 0], c_refs[g][:, 1]
            sc = jnp.sum(kc * q[None], axis=-1, keepdims=True) + bias_ref[g]
            s_new = jnp.sum(kn * q, axis=-1, keepdims=True) + b0_ref[0]
            mx = jnp.maximum(jnp.max(sc, axis=0), s_new)
            e = jnp.exp(sc - mx[None])
            e_new = jnp.exp(s_new - mx)
            den = jnp.sum(e, axis=0) + e_new
            outs.append((jnp.sum(e * vc, axis=0) + e_new * vn) / den)
            lses.append(mx + jnp.log(den))
        mx = functools.reduce(jnp.maximum, lses)
        ws = [jnp.exp(l - mx) for l in lses]
        tot = functools.reduce(lambda a, b: a + b, ws)
        o_ref[0] = functools.reduce(lambda a, b: a + b, [(ws[g] / tot) * outs[g] for g in range(ng)])

    full = lambda shape: pl.BlockSpec(shape, lambda b: (0,) * len(shape))
    out = pl.pallas_call(
        body,
        grid=(bd,),
        in_specs=[full((ng, nk, heads, 1)), full((1, heads, 1)),
                  pl.BlockSpec((1, 3 * ng, heads, head_dim), lambda b: (b, 0, 0, 0))]
        + [pl.BlockSpec((None, nk, None, 2, heads, head_dim), lambda b: (b, 0, 0, 0, 0, 0)) for _ in range(ng)],
        out_specs=pl.BlockSpec((1, heads, head_dim), lambda b: (b, 0, 0)),
        out_shape=jax.ShapeDtypeStruct((bd, heads, head_dim), F32),
        compiler_params=_params(("arbitrary",), 32),
    )(bias, bias0, q4, *views)
    return out.reshape(bd, heads * head_dim)


def _block_rank_select(gates, n_sel):
    sels = []
    nblk = len(gates)
    for k in range(nblk):
        rank = jnp.zeros(gates[k].shape, F32)
        for m in range(nblk):
            if m == k:
                continue
            beats = (gates[m] >= gates[k]) if m < k else (gates[m] > gates[k])
            rank = rank + beats.astype(F32)
        sels.append(rank < n_sel)
    return sels


def _moba_prompt(qkv, table, *, batch, seq, heads, kv_heads):
    d = B_HEAD_DIM
    blk = MOBA_BLOCK
    nblk = seq // blk
    grp = heads // kv_heads
    n_sel = min(MOBA_TOPK, nblk)
    scale = d ** -0.5
    r = np.arange(blk)[:, None]
    c = np.arange(blk)[None, :]
    bkt = jnp.asarray(np.stack([_bucket_np(db * blk + r - c) for db in range(nblk)]))

    def body(table_ref, bkt_ref, q_ref, k_ref, v_ref, o_ref, bias_ref, ka_ref, va_ref):
        h = pl.program_id(0)

        @pl.when(pl.program_id(1) == 0)
        def _():
            row = lax.broadcasted_iota(I32, (blk, blk), 0)
            col = lax.broadcasted_iota(I32, (blk, blk), 1)
            for db in range(nblk):
                bk = bkt_ref[db]
                t = lax.fori_loop(0, N_BUCKETS, lambda k, t: jnp.where(bk == k, table_ref[k, h], t),
                                  jnp.zeros((blk, blk), F32))
                if db == 0:
                    t = jnp.where(col > row, NEG_INF, t)
                bias_ref[db] = t

        lane = lax.broadcasted_iota(I32, (blk, d), 1)
        row16 = lax.broadcasted_iota(I32, (16, d), 0)
        kmean = jnp.zeros((16, d), F32)
        for n in range(nblk):
            rows = slice(n * blk, (n + 1) * blk)
            kn = k_ref[rows, :]
            ka_ref[rows, 0:d] = kn.astype(BF16)
            ka_ref[rows, d:2 * d] = jnp.where(lane == n, 1.0, 0.0).astype(BF16)
            va_ref[rows, 0:d] = v_ref[rows, :].astype(BF16)
            va_ref[rows, d:2 * d] = jnp.where(lane == 0, 1.0, 0.0).astype(BF16)
            kmean = jnp.where(row16 == n, jnp.mean(kn, axis=0, keepdims=True), kmean)
        k1, k2, k3 = _split3(kmean)
        row128 = lax.broadcasted_iota(I32, (LANES, blk), 0)

        for i in range(nblk):
            q = q_ref[i * blk:(i + 1) * blk, :]
            qa = (q * scale).astype(BF16)
            ranked = i > n_sel
            if ranked:
                q1, q2, q3 = _split3(q)
                gate_t = (_dot_nt(k1, q1) + _dot_nt(k1, q2) + _dot_nt(k2, q1)
                          + _dot_nt(k2, q2) + _dot_nt(k1, q3) + _dot_nt(k3, q1))
                sels = _block_rank_select([gate_t[n:n + 1, :] for n in range(i)], n_sel)
                sel_t = jnp.where(row128 == i, 1.0, 0.0)
                for n in range(i):
                    sel_t = jnp.where((row128 == n) & sels[n], 1.0, sel_t)
                sel_neg = jnp.where(sel_t.T > 0.5, 0.0, NEG_INF).astype(BF16)
                qa = jnp.concatenate([qa, sel_neg], axis=1)
            acc = None
            mx = None
            for n in range(i + 1):
                rows = slice(n * blk, (n + 1) * blk)
                kk = ka_ref[rows, :] if ranked else ka_ref[rows, 0:d]
                sc = _dot_nt(qa, kk) + bias_ref[i - n]
                blk_max = jnp.max(sc, axis=-1, keepdims=True)
                if n == 0:
                    mx = blk_max
                    acc = _dot(jnp.exp(sc - mx).astype(BF16), va_ref[rows, :])
                else:
                    m_new = jnp.maximum(mx, blk_max)
                    acc = jnp.exp(mx - m_new) * acc + _dot(jnp.exp(sc - m_new).astype(BF16), va_ref[rows, :])
                    mx = m_new
            o_ref[i * blk:(i + 1) * blk, :] = (acc[:, 0:d] / acc[:, d:d + 1]).astype(o_ref.dtype)

    return pl.pallas_call(
        body,
        grid=(heads, batch),
        in_specs=[
            pl.BlockSpec(memory_space=pltpu.SMEM),
            pl.BlockSpec((nblk, blk, blk), lambda h, b: (0, 0, 0)),
            pl.BlockSpec((seq, d), lambda h, b: (b, h)),
            pl.BlockSpec((seq, d), lambda h, b: (b, heads + h // grp)),
            pl.BlockSpec((seq, d), lambda h, b: (b, heads + kv_heads + h // grp)),
        ],
        out_specs=pl.BlockSpec((seq, d), lambda h, b: (b, h)),
        out_shape=jax.ShapeDtypeStruct((batch * seq, heads * d), BF16),
        scratch_shapes=[pltpu.VMEM((nblk, blk, blk), F32), pltpu.VMEM((seq, 2 * d), BF16),
                        pltpu.VMEM((seq, 2 * d), BF16)],
        compiler_params=_params(("arbitrary", "arbitrary"), 40),
    )(table, bkt, qkv, qkv, qkv)


def _moba_sample(q, k_new, v_new, cache, page_table, table, *, heads, kv_heads):
    bd = q.shape[0]
    d = B_HEAD_DIM
    blk = MOBA_BLOCK
    n_pages = page_table.shape[1]
    past = n_pages * PAGE_SIZE
    nblk = past // blk
    rpt = 2 * kv_heads
    assert blk == 2 * PAGE_SIZE and past % blk == 0 and rpt == 8
    grp = heads // kv_heads
    n_sel = min(MOBA_TOPK, nblk + 1)
    scale = d ** -0.5
    page_rows = PAGE_SIZE * rpt
    cols = blk * rpt
    pages = cache.reshape(cache.shape[0], page_rows, d)
    oh = np.zeros((LANES, past), np.float32)
    oh[_bucket_np(past - np.arange(past)), np.arange(past)] = 1.0
    onehot_t = jnp.asarray(oh, BF16)
    ex = np.zeros((blk, cols), np.float32)
    ex[np.arange(cols) // rpt, np.arange(cols)] = 1.0
    expand = jnp.asarray(ex, BF16)
    t_t = jnp.zeros((heads, LANES), F32).at[:, :N_BUCKETS].set(table.astype(F32).T)

    def body(pt_ref, tt_ref, oh_ref, ex_ref, q_ref, kn_ref, vn_ref, p0_ref, p1_ref, o_ref,
             bias_ref, m_ref, l_ref, acc_ref, gate_ref):
        b = pl.program_id(0)
        n = pl.program_id(1)

        @pl.when((b == 0) & (n == 0))
        def _():
            t1, t2, t3 = _split3(tt_ref[...])
            full = _dot(t1, oh_ref[...]) + _dot(t2, oh_ref[...]) + _dot(t3, oh_ref[...])
            col = lax.broadcasted_iota(I32, (heads, cols), 1)
            row = lax.broadcasted_iota(I32, (heads, cols), 0)
            own_k_row = (col % rpt) == (row // grp)
            for k in range(nblk):
                f1, f2, f3 = _split3(full[:, k * blk:(k + 1) * blk])
                bias_ref[k] = jnp.where(own_k_row, _dot(f1, ex_ref[...]) + _dot(f2, ex_ref[...])
                                        + _dot(f3, ex_ref[...]), NEG_INF)

        qv = q_ref[0]
        qb = (qv * scale).astype(BF16)
        x = jnp.concatenate([p0_ref[0], p1_ref[0]], axis=0)
        xb = x.astype(BF16)
        sc = _dot_nt(qb, xb) + bias_ref[n]
        mx = jnp.max(sc, axis=1, keepdims=True)
        e = jnp.exp(sc - mx)
        den = jnp.sum(e, axis=1, keepdims=True)
        acc = _dot(pltpu.roll(e, kv_heads, axis=1).astype(BF16), xb)
        ksum = jnp.sum(x.reshape(blk, rpt, d), axis=0)
        row_grp = lax.broadcasted_iota(I32, (heads, 1), 0) // grp
        gate = jnp.zeros((heads, 1), F32)
        for g in range(kv_heads):
            gg = jnp.sum(qv * (ksum[g:g + 1, :] * (1.0 / blk)), axis=1, keepdims=True)
            gate = jnp.where(row_grp == g, gg, gate)
        m_ref[n] = jnp.broadcast_to(mx, (heads, d))
        l_ref[n] = jnp.broadcast_to(den, (heads, d))
        gate_ref[n] = jnp.broadcast_to(gate, (heads, d))
        acc_ref[n] = acc

        @pl.when(n == nblk - 1)
        def _():
            gates = [gate_ref[k] for k in range(nblk)]
            sels = _block_rank_select(gates, n_sel)
            s_own = jnp.sum(qv * kn_ref[0], axis=1, keepdims=True) * scale + tt_ref[:, 0:1]
            big = jnp.broadcast_to(s_own, (heads, d))
            for k in range(nblk):
                big = jnp.maximum(big, jnp.where(sels[k], m_ref[k], NEG_INF))
            w_own = jnp.exp(s_own - big)
            tot = w_own
            out = w_own * vn_ref[0]
            for k in range(nblk):
                wk = jnp.where(sels[k], jnp.exp(m_ref[k] - big), 0.0)
                tot = tot + wk * l_ref[k]
                out = out + wk * acc_ref[k]
            o_ref[0] = out / tot

    def page_spec(which):
        return pl.BlockSpec((1, page_rows, d), lambda b, n, pt: (pt[b * n_pages + 2 * n + which], 0, 0))

    row3 = pl.BlockSpec((1, heads, d), lambda b, n, pt: (b, 0, 0))
    stat = pltpu.VMEM((nblk, heads, d), F32)
    return pl.pallas_call(
        body,
        grid_spec=pltpu.PrefetchScalarGridSpec(
            num_scalar_prefetch=1,
            grid=(bd, nblk),
            in_specs=[
                pl.BlockSpec((heads, LANES), lambda b, n, pt: (0, 0)),
                pl.BlockSpec((LANES, past), lambda b, n, pt: (0, 0)),
                pl.BlockSpec((blk, cols), lambda b, n, pt: (0, 0)),
                row3, row3, row3, page_spec(0), page_spec(1),
            ],
            out_specs=row3,
            scratch_shapes=[pltpu.VMEM((nblk, heads, cols), F32), stat, stat, stat, stat],
        ),
        out_shape=jax.ShapeDtypeStruct((bd, heads, d), F32),
        compiler_params=_params(("arbitrary", "arbitrary"), 32),
    )(page_table.reshape(-1).astype(I32), t_t, onehot_t, expand, q, k_new, v_new, pages, pages)


def _router(x, gamma, w_router, *, tm):
    m, d = x.shape
    ne = w_router.shape[1]
    w_pad = jnp.zeros((d, LANES), F32).at[:, :ne].set(w_router.astype(F32))

    def body(x_ref, g_ref, w_ref, h_ref, idx_ref, gate_ref):
        xv = x_ref[...]
        h = xv * lax.rsqrt(jnp.mean(xv * xv, axis=-1, keepdims=True) + RMS_EPS) * g_ref[...]
        h_ref[...] = h
        h1, h2, h3 = _split3(h)
        w1, w2, w3 = _split3(w_ref[...])
        lg = (_dot(h1, w1) + _dot(h1, w2) + _dot(h2, w1) + _dot(h2, w2) + _dot(h1, w3) + _dot(h3, w1))
        lane = lax.broadcasted_iota(I32, lg.shape, 1)
        lane_f = lane.astype(F32)
        lg = jnp.where(lane < ne, lg, -jnp.inf)
        v1 = jnp.max(lg, axis=-1, keepdims=True)
        i1 = jnp.min(jnp.where(lg == v1, lane_f, float(LANES)), axis=-1, keepdims=True)
        lg2 = jnp.where(lane_f == i1, -jnp.inf, lg)
        v2 = jnp.max(lg2, axis=-1, keepdims=True)
        i2 = jnp.min(jnp.where(lg2 == v2, lane_f, float(LANES)), axis=-1, keepdims=True)
        e2 = jnp.exp(v2 - v1)
        g1 = 1.0 / (1.0 + e2)
        g2 = e2 / (1.0 + e2)
        idx_ref[...] = jnp.where(lane == 0, i1, jnp.where(lane == 1, i2, 0.0)).astype(I32)
        gate_ref[...] = jnp.where(lane == 0, g1, jnp.where(lane == 1, g2, 0.0))

    return pl.pallas_call(
        body,
        grid=(m // tm,),
        in_specs=[pl.BlockSpec((tm, d), lambda i: (i, 0)), pl.BlockSpec((1, d), lambda i: (0, 0)),
                  pl.BlockSpec((d, LANES), lambda i: (0, 0))],
        out_specs=[pl.BlockSpec((tm, d), lambda i: (i, 0)), pl.BlockSpec((tm, LANES), lambda i: (i, 0)),
                   pl.BlockSpec((tm, LANES), lambda i: (i, 0))],
        out_shape=[jax.ShapeDtypeStruct((m, d), F32), jax.ShapeDtypeStruct((m, LANES), I32),
                   jax.ShapeDtypeStruct((m, LANES), F32)],
        compiler_params=_params(("arbitrary",), 40),
    )(x, gamma.reshape(1, d).astype(F32), w_pad)


def _combine(x3, ys3, dest, gamma, *, tc):
    n, s, _ = x3.shape
    d = s * LANES

    def body(dest_ref, x_ref, ys_ref, g_ref, o_ref, buf, sems):
        base = pl.program_id(0) * tc

        def copy(r, k):
            return pltpu.make_async_copy(ys_ref.at[dest_ref[2 * (base + r) + k]], buf.at[k, r], sems.at[k])

        def start(r, carry):
            copy(r, 0).start()
            copy(r, 1).start()
            return carry

        def wait(r, carry):
            copy(r, 0).wait()
            copy(r, 1).wait()
            return carry

        lax.fori_loop(0, tc, start, 0)
        lax.fori_loop(0, tc, wait, 0)
        xv = x_ref[...] + buf[0] + buf[1]
        ms = jnp.sum(jnp.sum(xv * xv, axis=2, keepdims=True), axis=1, keepdims=True) * (1.0 / d)
        o_ref[...] = xv * lax.rsqrt(ms + RMS_EPS) * g_ref[...]

    return pl.pallas_call(
        body,
        grid_spec=pltpu.PrefetchScalarGridSpec(
            num_scalar_prefetch=1,
            grid=(n // tc,),
            in_specs=[pl.BlockSpec((tc, s, LANES), lambda i, dr: (i, 0, 0)),
                      pl.BlockSpec(memory_space=pl.ANY),
                      pl.BlockSpec((1, s, LANES), lambda i, dr: (0, 0, 0))],
            out_specs=pl.BlockSpec((tc, s, LANES), lambda i, dr: (i, 0, 0)),
            scratch_shapes=[pltpu.VMEM((2, tc, s, LANES), F32), pltpu.SemaphoreType.DMA((2,))],
        ),
        out_shape=jax.ShapeDtypeStruct((n, s, LANES), F32),
        compiler_params=_params(("arbitrary",), 6 * tc * d * 4 // MIB + 8),
    )(dest, x3, ys3, gamma.reshape(1, s, LANES).astype(F32))


def _moe(x, gamma_ffn, w_router, wg, wu, wd, gamma_final, *, tm, tf, router_tile, combine_tile):
    n, d = x.shape
    ne = wg.shape[0]
    h, idx, gates = _router(x, gamma_ffn, w_router, tm=router_tile)
    e_flat = idx[:, :TOP_K].reshape(-1)
    onehot = (e_flat[:, None] == jnp.arange(ne, dtype=I32)[None, :]).astype(I32)
    csum = jnp.cumsum(onehot, axis=0)
    rank = jnp.sum((csum - onehot) * onehot, axis=1)
    counts = csum[-1]
    padded = (counts + tm - 1) // tm * tm
    pad_end = jnp.cumsum(padded)
    dest = ((pad_end - padded)[e_flat] + rank).astype(I32)
    n_tiles = -(-(n * TOP_K + ne * (tm - 1)) // tm)
    n_rows = n_tiles * tm
    tile_expert = jnp.minimum(
        jnp.searchsorted(pad_end, jnp.arange(n_tiles, dtype=I32) * tm, side='right'), ne - 1).astype(I32)
    n_used = (pad_end[-1] // tm).astype(I32).reshape(1)
    row_tok = jnp.zeros((n_rows,), I32).at[dest].set(jnp.arange(n * TOP_K, dtype=I32) // TOP_K)
    row_gate = jnp.zeros((n_rows,), F32).at[dest].set(gates[:, :TOP_K].reshape(-1))

    s = d // LANES
    ys = _ffn(h.reshape(n * s, LANES), wg, wu, wd, tile_expert, n_used, row_gate=row_gate, row_tok=row_tok,
              add_res=False, tm=tm, tf=tf, n_rows=n_rows)
    y = _combine(x.reshape(n, s, LANES), ys.reshape(n_rows, s, LANES), dest, gamma_final, tc=combine_tile)
    return y.reshape(n, d)


def kernel(x_prompt, x_sample, cache_win0_kv, cache_win1_kv, cache_win2_kv, cache_moba_kv, page_table, rel_bias, norm_mix, norm_ffn, norm_final, w_qkv_dil, w_o_dil, w_qkv_moba, w_o_moba, w_gate_dense, w_up_dense, w_down_dense, w_router, w_gate_moe, w_up_moe, w_down_moe):
    bp, sp, d = x_prompt.shape
    bd = x_sample.shape[0]
    n_p = bp * sp
    n = n_p + bd
    assert x_sample.shape[1] == 1 and n % ROW_TILE == 0
    table = rel_bias.astype(F32)
    x0 = jnp.concatenate([x_prompt.reshape(n_p, d), x_sample.reshape(bd, d)], axis=0)

    ng = len(DIL_GROUPS)
    qkv_a = _matmul(x0, w_qkv_dil[0].astype(BF16), gamma=norm_mix[0], tm=ROW_TILE, tn=1024)
    o_p = _dil_prompt(qkv_a, table, batch=bp, seq=sp, heads=A_SLOTS, head_dim=A_HEAD_DIM)
    caches = [c[0] for c in (cache_win0_kv, cache_win1_kv, cache_win2_kv)]
    o_s = _dil_sample(qkv_a[n_p:], caches, table, heads=A_SLOTS, head_dim=A_HEAD_DIM)
    o_a = jnp.concatenate([o_p, o_s.astype(BF16)], axis=0)
    x1 = _matmul(o_a, w_o_dil[0].astype(BF16), residual=x0, tm=ROW_TILE, tn=1024)
    dense_tiles = n // ROW_TILE
    x2 = _ffn(x1, w_gate_dense.astype(BF16), w_up_dense.astype(BF16), w_down_dense.astype(BF16),
              jnp.zeros((dense_tiles,), I32), jnp.full((1,), dense_tiles, I32),
              gamma=norm_ffn[0], add_res=True, tm=ROW_TILE, tf=FF_TILE)

    qd, kd = B_HEADS * B_HEAD_DIM, B_KV_HEADS * B_HEAD_DIM
    qkv_b = _matmul(x2, w_qkv_moba[0].astype(BF16), gamma=norm_mix[1], tm=ROW_TILE, tn=1024)
    m_p = _moba_prompt(qkv_b, table, batch=bp, seq=sp, heads=B_HEADS, kv_heads=B_KV_HEADS)
    qkv_bs = qkv_b[n_p:]
    grp = B_HEADS // B_KV_HEADS
    q_s = qkv_bs[:, :qd].reshape(bd, B_HEADS, B_HEAD_DIM)
    k_s = qkv_bs[:, qd:qd + kd].reshape(bd, B_KV_HEADS, B_HEAD_DIM)
    v_s = qkv_bs[:, qd + kd:].reshape(bd, B_KV_HEADS, B_HEAD_DIM)
    m_s = _moba_sample(q_s, jnp.repeat(k_s, grp, axis=1), jnp.repeat(v_s, grp, axis=1), cache_moba_kv[0],
                       page_table, table, heads=B_HEADS, kv_heads=B_KV_HEADS)
    o_b = jnp.concatenate([m_p, m_s.reshape(bd, qd).astype(BF16)], axis=0)
    x3 = _matmul(o_b, w_o_moba[0].astype(BF16), residual=x2, tm=ROW_TILE, tn=1024)
    y = _moe(x3, norm_ffn[1], w_router[0], w_gate_moe[0].astype(BF16), w_up_moe[0].astype(BF16),
             w_down_moe[0].astype(BF16), norm_final,
             tm=MOE_ROW_TILE, tf=FF_TILE, router_tile=ROW_TILE, combine_tile=COMBINE_TILE)

    y_prompt = y[:n_p].reshape(bp, sp, d)
    y_sample = y[n_p:].reshape(bd, 1, d)
    qa_p = qkv_a[:n_p].reshape(bp, sp, 3, ng, A_SLOTS, A_HEAD_DIM)
    qa_s = qkv_a[n_p:].reshape(bd, 1, 3, ng, A_SLOTS, A_HEAD_DIM)
    win_p, win_s = [], []
    for g, (window, _) in enumerate(DIL_GROUPS):
        keep = min(window, sp)
        win_p.append(jnp.stack([qa_p[:, sp - keep:, 1, g], qa_p[:, sp - keep:, 2, g]], axis=2)[None])
        win_s.append(jnp.stack([qa_s[:, :, 1, g], qa_s[:, :, 2, g]], axis=2)[None])
    kv_p = qkv_b[:n_p, qd:].reshape(bp, sp // PAGE_SIZE, PAGE_SIZE, 2, B_KV_HEADS, B_HEAD_DIM)[None]
    kv_s = qkv_bs[:, qd:].reshape(bd, 1, 2, B_KV_HEADS, B_HEAD_DIM)[None]
    return (y_prompt, y_sample, win_p[0], win_p[1], win_p[2], win_s[0], win_s[1], win_s[2], kv_p, kv_s)
```

```python
import functools
import math

import numpy as np
import jax
import jax.numpy as jnp
from jax import lax
from jax.experimental import pallas as pl
from jax.experimental.pallas import tpu as pltpu

F32 = jnp.float32
BF16 = jnp.bfloat16
I32 = jnp.int32

RMS_EPS = 1e-6
NEG_INF = -1e30
MIB = 1024 * 1024

N_BUCKETS = 32
MAX_DISTANCE = 2048
DIL_GROUPS = ((128, 1), (512, 4), (2048, 16))
A_SLOTS = 16
A_HEAD_DIM = 64
B_HEADS = 16
B_KV_HEADS = 4
B_HEAD_DIM = 128
MOBA_BLOCK = 256
MOBA_TOPK = 3
PAGE_SIZE = 128
N_EXPERTS = 8
TOP_K = 2

LANES = 128

ROW_TILE = 688
MOE_ROW_TILE = 512
FF_TILE = 512
COMBINE_TILE = 384
W_SPLIT = 4


def _params(sem, vmem_mib):
    return pltpu.CompilerParams(dimension_semantics=sem, vmem_limit_bytes=vmem_mib * MIB)


def _bucket_np(dist):
    max_exact = N_BUCKETS // 2
    n = np.maximum(np.asarray(dist, np.int64), 0)
    nf = np.maximum(n, 1).astype(np.float32)
    scale = np.float32((N_BUCKETS - max_exact) / math.log(MAX_DISTANCE / max_exact))
    large = max_exact + (np.log(nf / np.float32(max_exact)) * scale).astype(np.int32)
    return np.where(n < max_exact, n, np.minimum(large, N_BUCKETS - 1)).astype(np.int32)


def _split3(t):
    t1 = t.astype(BF16)
    r1 = t - t1.astype(F32)
    t2 = r1.astype(BF16)
    r2 = r1 - t2.astype(F32)
    return t1, t2, r2.astype(BF16)


def _aligned(x, m):
    return x if isinstance(x, int) else pl.multiple_of(x, m)


def _dot(a, b):
    return jnp.dot(a, b, preferred_element_type=F32)


def _dot_nt(a, b):
    return lax.dot_general(a, b, (((1,), (1,)), ((), ())), preferred_element_type=F32)


def _matmul(x, w, gamma=None, residual=None, *, tm, tn):
    m, k = x.shape
    n = w.shape[1]
    prenorm = gamma is not None
    stage = prenorm or x.dtype != BF16
    has_res = residual is not None
    kc = k // W_SPLIT

    def body(*refs):
        it = iter(refs)
        x_ref = next(it)
        g_ref = next(it) if prenorm else None
        w_refs = [next(it) for _ in range(W_SPLIT)]
        r_ref = next(it) if has_res else None
        o_ref = next(it)
        hb_ref = next(it) if stage else None
        if stage:
            @pl.when(pl.program_id(1) == 0)
            def _():
                xv = x_ref[...].astype(F32)
                if prenorm:
                    xv = xv * lax.rsqrt(jnp.mean(xv * xv, axis=-1, keepdims=True) + RMS_EPS) * g_ref[...]
                hb_ref[...] = xv.astype(BF16)
        a_ref = hb_ref if stage else x_ref
        acc = _dot(a_ref[:, 0:kc], w_refs[0][...])
        for c in range(1, W_SPLIT):
            acc = acc + _dot(a_ref[:, c * kc:(c + 1) * kc], w_refs[c][...])
        if has_res:
            acc = acc + r_ref[...]
        o_ref[...] = acc

    in_specs = [pl.BlockSpec((tm, k), lambda i, j: (i, 0))]
    args = [x]
    if prenorm:
        in_specs.append(pl.BlockSpec((1, k), lambda i, j: (0, 0)))
        args.append(gamma.reshape(1, k).astype(F32))
    for c in range(W_SPLIT):
        in_specs.append(pl.BlockSpec((kc, tn), lambda i, j, c=c: (c, j)))
        args.append(w)
    if has_res:
        in_specs.append(pl.BlockSpec((tm, tn), lambda i, j: (i, j)))
        args.append(residual)
    scratch = [pltpu.VMEM((tm, k), BF16)] if stage else []
    vmem = 2 * (tm * k * x.dtype.itemsize + k * tn * 2 + tm * tn * 4 * (2 if has_res else 1)) + tm * k * 2
    return pl.pallas_call(
        body,
        grid=(m // tm, n // tn),
        in_specs=in_specs,
        out_specs=pl.BlockSpec((tm, tn), lambda i, j: (i, j)),
        out_shape=jax.ShapeDtypeStruct((m, n), F32),
        scratch_shapes=scratch,
        compiler_params=_params(("arbitrary", "arbitrary"), vmem // MIB + 8),
    )(*args)


def _ffn(x, wg, wu, wd, tile_expert, n_used, gamma=None, row_gate=None, row_tok=None, *, add_res, tm, tf,
         n_rows=None):
    gather = row_tok is not None
    d = wg.shape[1]
    s = d // LANES
    m = n_rows if gather else x.shape[0]
    ff = wg.shape[2]
    nj = ff // tf
    n_tiles = m // tm
    prenorm = gamma is not None
    use_gate = row_gate is not None
    n_pf = 3 if gather else 2
    dc, fc = d // W_SPLIT, tf // W_SPLIT
    assert not (gather and (add_res or prenorm))

    def body(*refs):
        te_ref, nu_ref = refs[0], refs[1]
        tok_ref = refs[2] if gather else None
        it = iter(refs[n_pf:])
        x_ref = next(it)
        g_ref = next(it) if prenorm else None
        wg_refs = [next(it) for _ in range(W_SPLIT)]
        wu_refs = [next(it) for _ in range(W_SPLIT)]
        wd_refs = [next(it) for _ in range(W_SPLIT)]
        rg_ref = next(it) if use_gate else None
        o_ref = next(it)
        hb_ref = next(it)
        xg_ref, sems = (next(it), next(it)) if gather else (None, None)
        i = pl.program_id(0)
        j = pl.program_id(1)
        active = i < nu_ref[0]

        def row_copy(tile, slot, r):
            tok = tok_ref[tile * tm + r]
            return pltpu.make_async_copy(
                x_ref.at[pl.ds(_aligned(tok * s, s), s), :],
                xg_ref.at[slot, pl.ds(_aligned(r * s, s), s), :],
                sems.at[slot])

        def issue(tile, slot):
            def one(r, carry):
                row_copy(tile, slot, r).start()
                return carry
            lax.fori_loop(0, tm, one, 0, unroll=8)

        def drain(tile, slot):
            def one(r, carry):
                row_copy(tile, slot, r).wait()
                return carry
            lax.fori_loop(0, tm, one, 0, unroll=8)

        @pl.when(jnp.logical_and(active, j == 0))
        def _():
            if gather:
                slot = i % 2

                @pl.when(i == 0)
                def _():
                    issue(0, 0)

                drain(i, slot)
                for c in range(s):
                    hb_ref[:, c * LANES:(c + 1) * LANES] = xg_ref[slot, pl.ds(c, tm, stride=s), :].astype(BF16)

                @pl.when(i + 1 < nu_ref[0])
                def _():
                    issue(i + 1, 1 - slot)
            else:
                xv = x_ref[...]
                if prenorm:
                    xv = xv * lax.rsqrt(jnp.mean(xv * xv, axis=-1, keepdims=True) + RMS_EPS) * g_ref[...]
                hb_ref[...] = xv.astype(BF16)

        @pl.when(j == 0)
        def _():
            o_ref[...] = jnp.zeros_like(o_ref)

        @pl.when(active)
        def _():
            gt = _dot(hb_ref[:, 0:dc], wg_refs[0][0])
            up = _dot(hb_ref[:, 0:dc], wu_refs[0][0])
            for c in range(1, W_SPLIT):
                gt = gt + _dot(hb_ref[:, c * dc:(c + 1) * dc], wg_refs[c][0])
                up = up + _dot(hb_ref[:, c * dc:(c + 1) * dc], wu_refs[c][0])
            a = (gt / (1.0 + jnp.exp(-gt)) * up).astype(BF16)
            y = _dot(a[:, 0:fc], wd_refs[0][0])
            for c in range(1, W_SPLIT):
                y = y + _dot(a[:, c * fc:(c + 1) * fc], wd_refs[c][0])
            o_ref[...] += y

        @pl.when(jnp.logical_and(active, j == nj - 1))
        def _():
            r = o_ref[...]
            if use_gate:
                r = r * rg_ref[...]
            if add_res:
                r = r + x_ref[...]
            o_ref[...] = r

    def last_used(i, pf):
        return jnp.minimum(i, pf[1][0] - 1)

    def col_of(i, j, pf):
        return jnp.where(i < pf[1][0], j, nj - 1)

    if gather:
        in_specs = [pl.BlockSpec(memory_space=pl.ANY)]
    else:
        in_specs = [pl.BlockSpec((tm, d), lambda i, j, *pf: (last_used(i, pf), 0))]
    args = [x]
    if prenorm:
        in_specs.append(pl.BlockSpec((1, d), lambda i, j, *pf: (0, 0)))
        args.append(gamma.reshape(1, d).astype(F32))
    for wmat in (wg, wu):
        for c in range(W_SPLIT):
            in_specs.append(pl.BlockSpec(
                (1, dc, tf), lambda i, j, *pf, c=c: (pf[0][last_used(i, pf)], c, col_of(i, j, pf))))
            args.append(wmat)
    for c in range(W_SPLIT):
        in_specs.append(pl.BlockSpec(
            (1, fc, d), lambda i, j, *pf, c=c: (pf[0][last_used(i, pf)], col_of(i, j, pf) * W_SPLIT + c, 0)))
        args.append(wd)
    if use_gate:
        in_specs.append(pl.BlockSpec((tm, 1), lambda i, j, *pf: (last_used(i, pf), 0)))
        args.append(row_gate.reshape(m, 1))
    scratch = [pltpu.VMEM((tm, d), BF16)]
    vmem = 2 * (3 * d * tf * 2 + tm * d * 4 + (tm * LANES * 4 if use_gate else 0)) + tm * d * 2 + 2 * tm * d * 4
    if gather:
        scratch += [pltpu.VMEM((2, tm * s, LANES), F32), pltpu.SemaphoreType.DMA((2,))]
    prefetch = (tile_expert, n_used) + ((row_tok,) if gather else ())
    return pl.pallas_call(
        body,
        grid_spec=pltpu.PrefetchScalarGridSpec(
            num_scalar_prefetch=n_pf,
            grid=(n_tiles, nj),
            in_specs=in_specs,
            out_specs=pl.BlockSpec((tm, d), lambda i, j, *pf: (i, 0)),
            scratch_shapes=scratch,
        ),
        out_shape=jax.ShapeDtypeStruct((m, d), F32),
        compiler_params=_params(("arbitrary", "arbitrary"), vmem // MIB + 10),
    )(*prefetch, *args)


def _dil_bucket_maps(w, dil):
    i = np.arange(w)[:, None]
    j = np.arange(2 * w)[None, :]
    delta = i + w - j
    valid = (delta >= 0) & (delta <= w)
    bk = np.where(valid, _bucket_np(delta * dil), -1).astype(np.int32)
    first = np.where(j >= w, bk, -1).astype(np.int32)
    return np.stack([first, bk], axis=0)


def _dil_prompt(qkv, table, *, batch, seq, heads, head_dim):
    hd = heads * head_dim
    ng = len(DIL_GROUPS)
    nhp = hd // LANES
    w = DIL_GROUPS[0][0] // DIL_GROUPS[0][1]
    assert all(win // dil == w and seq % (dil * w) == 0 for win, dil in DIL_GROUPS) and 2 * head_dim == LANES
    n_iter = seq // w
    scale = head_dim ** -0.5
    bkt = jnp.asarray(np.stack([_dil_bucket_maps(w, dil) for _, dil in DIL_GROUPS]))
    merge_rows = 256

    def body(table_ref, bkt_ref, *refs):
        q_refs, k_refs, v_refs = refs[0:ng], refs[ng:2 * ng], refs[2 * ng:3 * ng]
        o_ref, bias_ref, og_ref, lg_ref = refs[3 * ng:]
        hp = pl.program_id(0)

        @pl.when(pl.program_id(1) == 0)
        def _():
            for g in range(ng):
                for v in range(2):
                    bk = bkt_ref[g, v]
                    for half in range(2):
                        t = lax.fori_loop(
                            0, N_BUCKETS, lambda k, t: jnp.where(bk == k, table_ref[k, 2 * hp + half], t),
                            jnp.zeros(bk.shape, F32))
                        bias_ref[g, v, half] = jnp.where(bk < 0, NEG_INF, t)

        lane = lax.broadcasted_iota(I32, (w, LANES), 1)
        low = lane < head_dim
        for g, (_, dil) in enumerate(DIL_GROUPS):
            nb = seq // (dil * w)

            def one_block(idx, carry, g=g, dil=dil, nb=nb):
                r = idx // nb
                n = idx - r * nb
                if dil == 1:
                    rows_c = pl.ds(_aligned(n * w, w), w)
                    rows_p = pl.ds(_aligned(jnp.maximum(n - 1, 0) * w, w), w)
                else:
                    rows_c = pl.ds(n * (w * dil) + r, w, stride=dil)
                    rows_p = pl.ds(jnp.maximum(n - 1, 0) * (w * dil) + r, w, stride=dil)
                q2 = q_refs[g][rows_c, :] * scale
                k2 = jnp.concatenate([k_refs[g][rows_p, :], k_refs[g][rows_c, :]], axis=0).astype(BF16)
                v2 = jnp.concatenate([v_refs[g][rows_p, :], v_refs[g][rows_c, :]], axis=0).astype(BF16)
                variant = jnp.minimum(n, 1)
                outs, lses = [], []
                for half in range(2):
                    keep = low if half == 0 else jnp.logical_not(low)
                    qm = jnp.where(keep, q2, 0.0).astype(BF16)
                    sc = _dot_nt(qm, k2) + bias_ref[g, variant, half]
                    mx = jnp.max(sc, axis=-1, keepdims=True)
                    e = jnp.exp(sc - mx)
                    den = jnp.sum(e, axis=-1, keepdims=True)
                    outs.append(_dot(e.astype(BF16), v2) / den)
                    lses.append(mx + jnp.log(den))
                og_ref[g, rows_c, :] = jnp.where(low, outs[0], outs[1])
                lg_ref[g, rows_c, :] = jnp.where(low, lses[0], lses[1])
                return carry

            lax.fori_loop(0, n_iter, one_block, 0, unroll=4)

        def merge(c, carry):
            rows = pl.ds(_aligned(c * merge_rows, merge_rows), merge_rows)
            ls = [lg_ref[g, rows, :] for g in range(ng)]
            mx = functools.reduce(jnp.maximum, ls)
            ws = [jnp.exp(l - mx) for l in ls]
            num = functools.reduce(lambda a, b: a + b, [ws[g] * og_ref[g, rows, :] for g in range(ng)])
            o_ref[rows, :] = (num / functools.reduce(lambda a, b: a + b, ws)).astype(o_ref.dtype)
            return carry

        lax.fori_loop(0, seq // merge_rows, merge, 0)

    def blk(which, g):
        return pl.BlockSpec((seq, LANES), lambda hp, b: (b, (which * ng + g) * nhp + hp))

    in_specs = [pl.BlockSpec(memory_space=pltpu.SMEM),
                pl.BlockSpec((ng, 2, w, 2 * w), lambda hp, b: (0, 0, 0, 0))]
    in_specs += [blk(which, g) for which in range(3) for g in range(ng)]
    return pl.pallas_call(
        body,
        grid=(nhp, batch),
        in_specs=in_specs,
        out_specs=pl.BlockSpec((seq, LANES), lambda hp, b: (b, hp)),
        out_shape=jax.ShapeDtypeStruct((batch * seq, hd), BF16),
        scratch_shapes=[pltpu.VMEM((ng, 2, 2, w, 2 * w), F32), pltpu.VMEM((ng, seq, LANES), F32),
                        pltpu.VMEM((ng, seq, LANES), F32)],
        compiler_params=_params(("arbitrary", "arbitrary"), 48),
    )(table, bkt, *([qkv] * (3 * ng)))


def _dil_sample_group(q4, cache, table, g, *, heads, head_dim, head_chunk):
    bd = q4.shape[0]
    ng = len(DIL_GROUPS)
    window, dil = DIL_GROUPS[g]
    assert cache.shape[1] == window and heads % head_chunk == 0
    kv_t = jnp.transpose(cache, (0, 2, 3, 4, 1))
    w_idx = np.arange(window)
    bias = jnp.where(jnp.asarray(w_idx % dil == 0)[None, :], table[_bucket_np(window - w_idx)].T, NEG_INF)
    bias0 = table[0][:, None]
    n_chunks = heads // head_chunk
    scale = head_dim ** -0.5

    def body(bias_ref, b0_ref, q_ref, *refs):
        k_refs, v_refs = refs[0:head_chunk], refs[head_chunk:2 * head_chunk]
        o_ref, l_ref = refs[2 * head_chunk:]
        h0 = pl.program_id(1) * head_chunk
        row = lax.broadcasted_iota(I32, (head_chunk, head_dim), 0)
        o_acc = jnp.zeros((head_chunk, head_dim), F32)
        l_acc = jnp.zeros((head_chunk, head_dim), F32)
        for hh in range(head_chunk):
            hrow = pl.ds(h0 + hh, 1)
            q = q_ref[0, g, hrow, :] * scale
            kn, vn = q_ref[0, ng + g, hrow, :], q_ref[0, 2 * ng + g, hrow, :]
            kt = k_refs[hh][...].astype(BF16)
            vt = v_refs[hh][...].astype(BF16)
            sc = _dot(jnp.broadcast_to(q, (8, head_dim)).astype(BF16), kt)[0:1] + bias_ref[hrow, :]
            s_new = jnp.sum(q * kn, axis=1, keepdims=True) + b0_ref[hrow, :]
            mx = jnp.maximum(jnp.max(sc, axis=1, keepdims=True), s_new)
            e = jnp.exp(sc - mx)
            e_new = jnp.exp(s_new - mx)
            den = jnp.sum(e, axis=1, keepdims=True) + e_new
            pv = _dot_nt(jnp.broadcast_to(e, (8, window)).astype(BF16), vt)[0:1]
            o_acc = jnp.where(row == hh, (pv + e_new * vn) / den, o_acc)
            l_acc = jnp.where(row == hh, mx + jnp.log(den), l_acc)
        o_ref[0, 0] = o_acc
        l_ref[0, 0] = l_acc

    def head_spec(which, hh):
        return pl.BlockSpec((None, None, None, head_dim, window),
                            lambda b, hc: (b, which, hc * head_chunk + hh, 0, 0))

    out_spec = pl.BlockSpec((1, 1, head_chunk, head_dim), lambda b, hc: (b, hc, 0, 0))
    out_sds = jax.ShapeDtypeStruct((bd, n_chunks, head_chunk, head_dim), F32)
    o, lse = pl.pallas_call(
        body,
        grid=(bd, n_chunks),
        in_specs=[pl.BlockSpec((heads, window), lambda b, hc: (0, 0)),
                  pl.BlockSpec((heads, 1), lambda b, hc: (0, 0)),
                  pl.BlockSpec((1, 3 * ng, heads, head_dim), lambda b, hc: (b, 0, 0, 0))]
        + [head_spec(0, hh) for hh in range(head_chunk)] + [head_spec(1, hh) for hh in range(head_chunk)],
        out_specs=[out_spec, out_spec],
        out_shape=[out_sds, out_sds],
        compiler_params=_params(("arbitrary", "arbitrary"), 32),
    )(bias, bias0, q4, *([kv_t] * (2 * head_chunk)))
    return o.reshape(bd, heads, head_dim), lse.reshape(bd, heads, head_dim)


def _dil_sample(qkv_s, caches, table, *, heads, head_dim):
    bd = qkv_s.shape[0]
    ng = len(DIL_GROUPS)
    q4 = qkv_s.reshape(bd, 3 * ng, heads, head_dim)
    outs, lses = [], []
    for g, c in enumerate(caches):
        chunk = max(1, min(heads, (4 * MIB) // (2 * head_dim * DIL_GROUPS[g][0] * 4)))
        o, lse = _dil_sample_group(q4, c, table, g, heads=heads, head_dim=head_dim, head_chunk=chunk)
        outs.append(o)
        lses.append(lse)

    def body(*refs):
        o_refs, l_refs, out_ref = refs[:ng], refs[ng:2 * ng], refs[2 * ng]
        ls = [r[...] for r in l_refs]
        mx = functools.reduce(jnp.maximum, ls)
        ws = [jnp.exp(l - mx) for l in ls]
        num = functools.reduce(lambda a, b: a + b, [wt * r[...] for wt, r in zip(ws, o_refs)])
        out_ref[...] = num / functools.reduce(lambda a, b: a + b, ws)

    spec = pl.BlockSpec((bd, heads, head_dim), lambda i: (0, 0, 0))
    out = pl.pallas_call(
        body,
        grid=(1,),
        in_specs=[spec] * (2 * ng),
        out_specs=spec,
        out_shape=jax.ShapeDtypeStruct((bd, heads, head_dim), F32),
        compiler_params=_params(("arbitrary",), 32),
    )(*outs, *lses)
    return out.reshape(bd, heads * head_dim)


def _block_rank_select(gates, n_sel):
    sels = []
    nblk = len(gates)
    for k in range(nblk):
        rank = jnp.zeros(gates[k].shape, F32)
        for m in range(nblk):
            if m == k:
                continue
            beats = (gates[m] >= gates[k]) if m < k else (gates[m] > gates[k])
            rank = rank + beats.astype(F32)
        sels.append(rank < n_sel)
    return sels


def _moba_prompt(qkv, table, *, batch, seq, heads, kv_heads):
    d = B_HEAD_DIM
    blk = MOBA_BLOCK
    nblk = seq // blk
    grp = heads // kv_heads
    n_sel = min(MOBA_TOPK, nblk)
    scale = d ** -0.5
    r = np.arange(blk)[:, None]
    c = np.arange(blk)[None, :]
    bkt = jnp.asarray(np.stack([_bucket_np(db * blk + r - c) for db in range(nblk)]))

    def body(table_ref, bkt_ref, q_ref, k_ref, v_ref, o_ref, bias_ref, ka_ref, va_ref):
        h = pl.program_id(0)

        @pl.when(pl.program_id(1) == 0)
        def _():
            row = lax.broadcasted_iota(I32, (blk, blk), 0)
            col = lax.broadcasted_iota(I32, (blk, blk), 1)
            for db in range(nblk):
                bk = bkt_ref[db]
                t = lax.fori_loop(0, N_BUCKETS, lambda k, t: jnp.where(bk == k, table_ref[k, h], t),
                                  jnp.zeros((blk, blk), F32))
                if db == 0:
                    t = jnp.where(col > row, NEG_INF, t)
                bias_ref[db] = t

        lane = lax.broadcasted_iota(I32, (blk, d), 1)
        row16 = lax.broadcasted_iota(I32, (16, d), 0)
        kmean = jnp.zeros((16, d), F32)
        for n in range(nblk):
            rows = slice(n * blk, (n + 1) * blk)
            kn = k_ref[rows, :]
            ka_ref[rows, 0:d] = kn.astype(BF16)
            ka_ref[rows, d:2 * d] = jnp.where(lane == n, 1.0, 0.0).astype(BF16)
            va_ref[rows, 0:d] = v_ref[rows, :].astype(BF16)
            va_ref[rows, d:2 * d] = jnp.where(lane == 0, 1.0, 0.0).astype(BF16)
            kmean = jnp.where(row16 == n, jnp.mean(kn, axis=0, keepdims=True), kmean)
        k1, k2, k3 = _split3(kmean)
        row128 = lax.broadcasted_iota(I32, (LANES, blk), 0)

        for i in range(nblk):
            q = q_ref[i * blk:(i + 1) * blk, :]
            qa = (q * scale).astype(BF16)
            ranked = i > n_sel
            if ranked:
                q1, q2, q3 = _split3(q)
                gate_t = (_dot_nt(k1, q1) + _dot_nt(k1, q2) + _dot_nt(k2, q1)
                          + _dot_nt(k2, q2) + _dot_nt(k1, q3) + _dot_nt(k3, q1))
                sels = _block_rank_select([gate_t[n:n + 1, :] for n in range(i)], n_sel)
                sel_t = jnp.where(row128 == i, 1.0, 0.0)
                for n in range(i):
                    sel_t = jnp.where((row128 == n) & sels[n], 1.0, sel_t)
                sel_neg = jnp.where(sel_t.T > 0.5, 0.0, NEG_INF).astype(BF16)
                qa = jnp.concatenate([qa, sel_neg], axis=1)
            acc = None
            mx = None
            for n in range(i + 1):
                rows = slice(n * blk, (n + 1) * blk)
                kk = ka_ref[rows, :] if ranked else ka_ref[rows, 0:d]
                sc = _dot_nt(qa, kk) + bias_ref[i - n]
                blk_max = jnp.max(sc, axis=-1, keepdims=True)
                if n == 0:
                    mx = blk_max
                    acc = _dot(jnp.exp(sc - mx).astype(BF16), va_ref[rows, :])
                else:
                    m_new = jnp.maximum(mx, blk_max)
                    acc = jnp.exp(mx - m_new) * acc + _dot(jnp.exp(sc - m_new).astype(BF16), va_ref[rows, :])
                    mx = m_new
            o_ref[i * blk:(i + 1) * blk, :] = (acc[:, 0:d] / acc[:, d:d + 1]).astype(o_ref.dtype)

    return pl.pallas_call(
        body,
        grid=(heads, batch),
        in_specs=[
            pl.BlockSpec(memory_space=pltpu.SMEM),
            pl.BlockSpec((nblk, blk, blk), lambda h, b: (0, 0, 0)),
            pl.BlockSpec((seq, d), lambda h, b: (b, h)),
            pl.BlockSpec((seq, d), lambda h, b: (b, heads + h // grp)),
            pl.BlockSpec((seq, d), lambda h, b: (b, heads + kv_heads + h // grp)),
        ],
        out_specs=pl.BlockSpec((seq, d), lambda h, b: (b, h)),
        out_shape=jax.ShapeDtypeStruct((batch * seq, heads * d), BF16),
        scratch_shapes=[pltpu.VMEM((nblk, blk, blk), F32), pltpu.VMEM((seq, 2 * d), BF16),
                        pltpu.VMEM((seq, 2 * d), BF16)],
        compiler_params=_params(("arbitrary", "arbitrary"), 40),
    )(table, bkt, qkv, qkv, qkv)


def _moba_sample(q, k_new, v_new, cache, page_table, table, *, heads, kv_heads):
    bd = q.shape[0]
    d = B_HEAD_DIM
    blk = MOBA_BLOCK
    n_pages = page_table.shape[1]
    past = n_pages * PAGE_SIZE
    nblk = past // blk
    rpt = 2 * kv_heads
    assert blk == 2 * PAGE_SIZE and past % blk == 0 and rpt == 8
    grp = heads // kv_heads
    n_sel = min(MOBA_TOPK, nblk + 1)
    scale = d ** -0.5
    page_rows = PAGE_SIZE * rpt
    cols = blk * rpt
    pages = cache.reshape(cache.shape[0], page_rows, d)
    oh = np.zeros((LANES, past), np.float32)
    oh[_bucket_np(past - np.arange(past)), np.arange(past)] = 1.0
    onehot_t = jnp.asarray(oh, BF16)
    ex = np.zeros((blk, cols), np.float32)
    ex[np.arange(cols) // rpt, np.arange(cols)] = 1.0
    expand = jnp.asarray(ex, BF16)
    t_t = jnp.zeros((heads, LANES), F32).at[:, :N_BUCKETS].set(table.astype(F32).T)

    sub = 2 if nblk % 2 == 0 else 1
    psplit = 2
    part_rows = page_rows // psplit
    n_parts = sub * 2 * psplit

    def body(pt_ref, tt_ref, oh_ref, ex_ref, q_ref, kn_ref, vn_ref, *refs):
        part_refs = refs[:n_parts]
        o_ref, bias_ref, m_ref, l_ref, acc_ref, gate_ref = refs[n_parts:]
        b = pl.program_id(0)
        step = pl.program_id(1)

        @pl.when((b == 0) & (step == 0))
        def _():
            t1, t2, t3 = _split3(tt_ref[...])
            full = _dot(t1, oh_ref[...]) + _dot(t2, oh_ref[...]) + _dot(t3, oh_ref[...])
            col = lax.broadcasted_iota(I32, (heads, cols), 1)
            row = lax.broadcasted_iota(I32, (heads, cols), 0)
            own_k_row = (col % rpt) == (row // grp)
            for k in range(nblk):
                f1, f2, f3 = _split3(full[:, k * blk:(k + 1) * blk])
                bias_ref[k] = jnp.where(own_k_row, _dot(f1, ex_ref[...]) + _dot(f2, ex_ref[...])
                                        + _dot(f3, ex_ref[...]), NEG_INF)

        qv = q_ref[0]
        qb = (qv * scale).astype(BF16)
        row_grp = lax.broadcasted_iota(I32, (heads, 1), 0) // grp
        for kb in range(sub):
            n = step * sub + kb
            parts = part_refs[kb * 2 * psplit:(kb + 1) * 2 * psplit]
            x = jnp.concatenate([p[0] for p in parts], axis=0)
            xb = x.astype(BF16)
            sc = _dot_nt(qb, xb) + bias_ref[n]
            mx = jnp.max(sc, axis=1, keepdims=True)
            e = jnp.exp(sc - mx)
            den = jnp.sum(e, axis=1, keepdims=True)
            acc = _dot(pltpu.roll(e, kv_heads, axis=1).astype(BF16), xb)
            ksum = jnp.sum(x.reshape(blk, rpt, d), axis=0)
            gate = jnp.zeros((heads, 1), F32)
            for g in range(kv_heads):
                gg = jnp.sum(qv * (ksum[g:g + 1, :] * (1.0 / blk)), axis=1, keepdims=True)
                gate = jnp.where(row_grp == g, gg, gate)
            m_ref[n] = jnp.broadcast_to(mx, (heads, d))
            l_ref[n] = jnp.broadcast_to(den, (heads, d))
            gate_ref[n] = jnp.broadcast_to(gate, (heads, d))
            acc_ref[n] = acc

        @pl.when(step == nblk // sub - 1)
        def _():
            gates = [gate_ref[k] for k in range(nblk)]
            sels = _block_rank_select(gates, n_sel)
            s_own = jnp.sum(qv * kn_ref[0], axis=1, keepdims=True) * scale + tt_ref[:, 0:1]
            big = jnp.broadcast_to(s_own, (heads, d))
            for k in range(nblk):
                big = jnp.maximum(big, jnp.where(sels[k], m_ref[k], NEG_INF))
            w_own = jnp.exp(s_own - big)
            tot = w_own
            out = w_own * vn_ref[0]
            for k in range(nblk):
                wk = jnp.where(sels[k], jnp.exp(m_ref[k] - big), 0.0)
                tot = tot + wk * l_ref[k]
                out = out + wk * acc_ref[k]
            o_ref[0] = out / tot

    def part_spec(which):
        page, part = which // psplit, which % psplit
        return pl.BlockSpec((1, part_rows, d),
                            lambda b, n, pt: (pt[b * n_pages + 2 * sub * n + page], part, 0))

    row3 = pl.BlockSpec((1, heads, d), lambda b, n, pt: (b, 0, 0))
    stat = pltpu.VMEM((nblk, heads, d), F32)
    return pl.pallas_call(
        body,
        grid_spec=pltpu.PrefetchScalarGridSpec(
            num_scalar_prefetch=1,
            grid=(bd, nblk // sub),
            in_specs=[
                pl.BlockSpec((heads, LANES), lambda b, n, pt: (0, 0)),
                pl.BlockSpec((LANES, past), lambda b, n, pt: (0, 0)),
                pl.BlockSpec((blk, cols), lambda b, n, pt: (0, 0)),
                row3, row3, row3] + [part_spec(which) for which in range(n_parts)],
            out_specs=row3,
            scratch_shapes=[pltpu.VMEM((nblk, heads, cols), F32), stat, stat, stat, stat],
        ),
        out_shape=jax.ShapeDtypeStruct((bd, heads, d), F32),
        compiler_params=_params(("arbitrary", "arbitrary"), 32),
    )(page_table.reshape(-1).astype(I32), t_t, onehot_t, expand, q, k_new, v_new, *([pages] * n_parts))


def _router(x, gamma, w_router, *, tm):
    m, d = x.shape
    ne = w_router.shape[1]
    w_pad = jnp.zeros((d, LANES), F32).at[:, :ne].set(w_router.astype(F32))

    def body(x_ref, g_ref, w_ref, h_ref, idx_ref, gate_ref):
        xv = x_ref[...]
        h = xv * lax.rsqrt(jnp.mean(xv * xv, axis=-1, keepdims=True) + RMS_EPS) * g_ref[...]
        h_ref[...] = h
        h1, h2, h3 = _split3(h)
        w1, w2, w3 = _split3(w_ref[...])
        lg = (_dot(h1, w1) + _dot(h1, w2) + _dot(h2, w1) + _dot(h2, w2) + _dot(h1, w3) + _dot(h3, w1))
        lane = lax.broadcasted_iota(I32, lg.shape, 1)
        lane_f = lane.astype(F32)
        lg = jnp.where(lane < ne, lg, -jnp.inf)
        v1 = jnp.max(lg, axis=-1, keepdims=True)
        i1 = jnp.min(jnp.where(lg == v1, lane_f, float(LANES)), axis=-1, keepdims=True)
        lg2 = jnp.where(lane_f == i1, -jnp.inf, lg)
        v2 = jnp.max(lg2, axis=-1, keepdims=True)
        i2 = jnp.min(jnp.where(lg2 == v2, lane_f, float(LANES)), axis=-1, keepdims=True)
        e2 = jnp.exp(v2 - v1)
        g1 = 1.0 / (1.0 + e2)
        g2 = e2 / (1.0 + e2)
        idx_ref[...] = jnp.where(lane == 0, i1, jnp.where(lane == 1, i2, 0.0)).astype(I32)
        gate_ref[...] = jnp.where(lane == 0, g1, jnp.where(lane == 1, g2, 0.0))

    return pl.pallas_call(
        body,
        grid=(m // tm,),
        in_specs=[pl.BlockSpec((tm, d), lambda i: (i, 0)), pl.BlockSpec((1, d), lambda i: (0, 0)),
                  pl.BlockSpec((d, LANES), lambda i: (0, 0))],
        out_specs=[pl.BlockSpec((tm, d), lambda i: (i, 0)), pl.BlockSpec((tm, LANES), lambda i: (i, 0)),
                   pl.BlockSpec((tm, LANES), lambda i: (i, 0))],
        out_shape=[jax.ShapeDtypeStruct((m, d), F32), jax.ShapeDtypeStruct((m, LANES), I32),
                   jax.ShapeDtypeStruct((m, LANES), F32)],
        compiler_params=_params(("arbitrary",), 40),
    )(x, gamma.reshape(1, d).astype(F32), w_pad)


def _combine(x3, ys3, dest, gamma, *, tc):
    n, s, _ = x3.shape
    d = s * LANES

    def body(dest_ref, x_ref, ys_ref, g_ref, o_ref, buf, sems):
        base = pl.program_id(0) * tc

        def copy(r, k):
            return pltpu.make_async_copy(ys_ref.at[dest_ref[2 * (base + r) + k]], buf.at[k, r], sems.at[k])

        def start(r, carry):
            copy(r, 0).start()
            copy(r, 1).start()
            return carry

        def wait(r, carry):
            copy(r, 0).wait()
            copy(r, 1).wait()
            return carry

        lax.fori_loop(0, tc, start, 0)
        lax.fori_loop(0, tc, wait, 0)
        xv = x_ref[...] + buf[0] + buf[1]
        ms = jnp.sum(jnp.sum(xv * xv, axis=2, keepdims=True), axis=1, keepdims=True) * (1.0 / d)
        o_ref[...] = xv * lax.rsqrt(ms + RMS_EPS) * g_ref[...]

    return pl.pallas_call(
        body,
        grid_spec=pltpu.PrefetchScalarGridSpec(
            num_scalar_prefetch=1,
            grid=(n // tc,),
            in_specs=[pl.BlockSpec((tc, s, LANES), lambda i, dr: (i, 0, 0)),
                      pl.BlockSpec(memory_space=pl.ANY),
                      pl.BlockSpec((1, s, LANES), lambda i, dr: (0, 0, 0))],
            out_specs=pl.BlockSpec((tc, s, LANES), lambda i, dr: (i, 0, 0)),
            scratch_shapes=[pltpu.VMEM((2, tc, s, LANES), F32), pltpu.SemaphoreType.DMA((2,))],
        ),
        out_shape=jax.ShapeDtypeStruct((n, s, LANES), F32),
        compiler_params=_params(("arbitrary",), 6 * tc * d * 4 // MIB + 8),
    )(dest, x3, ys3, gamma.reshape(1, s, LANES).astype(F32))


def _moe(x, gamma_ffn, w_router, wg, wu, wd, gamma_final, *, tm, tf, router_tile, combine_tile):
    n, d = x.shape
    ne = wg.shape[0]
    h, idx, gates = _router(x, gamma_ffn, w_router, tm=router_tile)
    e_flat = idx[:, :TOP_K].reshape(-1)
    onehot = (e_flat[:, None] == jnp.arange(ne, dtype=I32)[None, :]).astype(I32)
    csum = jnp.cumsum(onehot, axis=0)
    rank = jnp.sum((csum - onehot) * onehot, axis=1)
    counts = csum[-1]
    padded = (counts + tm - 1) // tm * tm
    pad_end = jnp.cumsum(padded)
    dest = ((pad_end - padded)[e_flat] + rank).astype(I32)
    n_tiles = -(-(n * TOP_K + ne * (tm - 1)) // tm)
    n_rows = n_tiles * tm
    tile_expert = jnp.minimum(
        jnp.searchsorted(pad_end, jnp.arange(n_tiles, dtype=I32) * tm, side='right'), ne - 1).astype(I32)
    n_used = (pad_end[-1] // tm).astype(I32).reshape(1)
    row_tok = jnp.zeros((n_rows,), I32).at[dest].set(jnp.arange(n * TOP_K, dtype=I32) // TOP_K)
    row_gate = jnp.zeros((n_rows,), F32).at[dest].set(gates[:, :TOP_K].reshape(-1))

    s = d // LANES
    ys = _ffn(h.reshape(n * s, LANES), wg, wu, wd, tile_expert, n_used, row_gate=row_gate, row_tok=row_tok,
              add_res=False, tm=tm, tf=tf, n_rows=n_rows)
    y = _combine(x.reshape(n, s, LANES), ys.reshape(n_rows, s, LANES), dest, gamma_final, tc=combine_tile)
    return y.reshape(n, d)


def kernel(x_prompt, x_sample, cache_win0_kv, cache_win1_kv, cache_win2_kv, cache_moba_kv, page_table, rel_bias, norm_mix, norm_ffn, norm_final, w_qkv_dil, w_o_dil, w_qkv_moba, w_o_moba, w_gate_dense, w_up_dense, w_down_dense, w_router, w_gate_moe, w_up_moe, w_down_moe):
    bp, sp, d = x_prompt.shape
    bd = x_sample.shape[0]
    n_p = bp * sp
    n = n_p + bd
    assert x_sample.shape[1] == 1 and n % ROW_TILE == 0
    table = rel_bias.astype(F32)
    x0 = jnp.concatenate([x_prompt.reshape(n_p, d), x_sample.reshape(bd, d)], axis=0)

    ng = len(DIL_GROUPS)
    qkv_a = _matmul(x0, w_qkv_dil[0].astype(BF16), gamma=norm_mix[0], tm=ROW_TILE, tn=1024)
    o_p = _dil_prompt(qkv_a, table, batch=bp, seq=sp, heads=A_SLOTS, head_dim=A_HEAD_DIM)
    caches = [c[0] for c in (cache_win0_kv, cache_win1_kv, cache_win2_kv)]
    o_s = _dil_sample(qkv_a[n_p:], caches, table, heads=A_SLOTS, head_dim=A_HEAD_DIM)
    o_a = jnp.concatenate([o_p, o_s.astype(BF16)], axis=0)
    x1 = _matmul(o_a, w_o_dil[0].astype(BF16), residual=x0, tm=ROW_TILE, tn=1024)
    dense_tiles = n // ROW_TILE
    x2 = _ffn(x1, w_gate_dense.astype(BF16), w_up_dense.astype(BF16), w_down_dense.astype(BF16),
              jnp.zeros((dense_tiles,), I32), jnp.full((1,), dense_tiles, I32),
              gamma=norm_ffn[0], add_res=True, tm=ROW_TILE, tf=FF_TILE)

    qd, kd = B_HEADS * B_HEAD_DIM, B_KV_HEADS * B_HEAD_DIM
    qkv_b = _matmul(x2, w_qkv_moba[0].astype(BF16), gamma=norm_mix[1], tm=ROW_TILE, tn=1024)
    m_p = _moba_prompt(qkv_b, table, batch=bp, seq=sp, heads=B_HEADS, kv_heads=B_KV_HEADS)
    qkv_bs = qkv_b[n_p:]
    grp = B_HEADS // B_KV_HEADS
    q_s = qkv_bs[:, :qd].reshape(bd, B_HEADS, B_HEAD_DIM)
    k_s = qkv_bs[:, qd:qd + kd].reshape(bd, B_KV_HEADS, B_HEAD_DIM)
    v_s = qkv_bs[:, qd + kd:].reshape(bd, B_KV_HEADS, B_HEAD_DIM)
    m_s = _moba_sample(q_s, jnp.repeat(k_s, grp, axis=1), jnp.repeat(v_s, grp, axis=1), cache_moba_kv[0],
                       page_table, table, heads=B_HEADS, kv_heads=B_KV_HEADS)
    o_b = jnp.concatenate([m_p, m_s.reshape(bd, qd).astype(BF16)], axis=0)
    x3 = _matmul(o_b, w_o_moba[0].astype(BF16), residual=x2, tm=ROW_TILE, tn=1024)
    y = _moe(x3, norm_ffn[1], w_router[0], w_gate_moe[0].astype(BF16), w_up_moe[0].astype(BF16),
             w_down_moe[0].astype(BF16), norm_final,
             tm=MOE_ROW_TILE, tf=FF_TILE, router_tile=ROW_TILE, combine_tile=COMBINE_TILE)

    y_prompt = y[:n_p].reshape(bp, sp, d)
    y_sample = y[n_p:].reshape(bd, 1, d)
    qa_p = qkv_a[:n_p].reshape(bp, sp, 3, ng, A_SLOTS, A_HEAD_DIM)
    qa_s = qkv_a[n_p:].reshape(bd, 1, 3, ng, A_SLOTS, A_HEAD_DIM)
    win_p, win_s = [], []
    for g, (window, _) in enumerate(DIL_GROUPS):
        keep = min(window, sp)
        win_p.append(jnp.stack([qa_p[:, sp - keep:, 1, g], qa_p[:, sp - keep:, 2, g]], axis=2)[None])
        win_s.append(jnp.stack([qa_s[:, :, 1, g], qa_s[:, :, 2, g]], axis=2)[None])
    kv_p = qkv_b[:n_p, qd:].reshape(bp, sp // PAGE_SIZE, PAGE_SIZE, 2, B_KV_HEADS, B_HEAD_DIM)[None]
    kv_s = qkv_bs[:, qd:].reshape(bd, 1, 2, B_KV_HEADS, B_HEAD_DIM)[None]
    return (y_prompt, y_sample, win_p[0], win_p[1], win_p[2], win_s[0], win_s[1], win_s[2], kv_p, kv_s)
```

```python
import functools
import math

import numpy as np
import jax
import jax.numpy as jnp
from jax import lax
from jax.experimental import pallas as pl
from jax.experimental.pallas import tpu as pltpu

F32 = jnp.float32
BF16 = jnp.bfloat16
I32 = jnp.int32

RMS_EPS = 1e-6
NEG_INF = -1e30
MIB = 1024 * 1024

N_BUCKETS = 32
MAX_DISTANCE = 2048
DIL_GROUPS = ((128, 1), (512, 4), (2048, 16))
A_SLOTS = 16
A_HEAD_DIM = 64
B_HEADS = 16
B_KV_HEADS = 4
B_HEAD_DIM = 128
MOBA_BLOCK = 256
MOBA_TOPK = 3
PAGE_SIZE = 128
N_EXPERTS = 8
TOP_K = 2

LANES = 128

ROW_TILE = 688
MOE_ROW_TILE = 768
FF_TILE = 512
COMBINE_TILE = 384
W_SPLIT = 1
DIL_SAMPLE_STEP_BYTES = 8 * MIB


def _params(sem, vmem_mib):
    return pltpu.CompilerParams(dimension_semantics=sem, vmem_limit_bytes=vmem_mib * MIB)


def _bucket_np(dist):
    max_exact = N_BUCKETS // 2
    n = np.maximum(np.asarray(dist, np.int64), 0)
    nf = np.maximum(n, 1).astype(np.float32)
    scale = np.float32((N_BUCKETS - max_exact) / math.log(MAX_DISTANCE / max_exact))
    large = max_exact + (np.log(nf / np.float32(max_exact)) * scale).astype(np.int32)
    return np.where(n < max_exact, n, np.minimum(large, N_BUCKETS - 1)).astype(np.int32)


def _split3(t):
    t1 = t.astype(BF16)
    r1 = t - t1.astype(F32)
    t2 = r1.astype(BF16)
    r2 = r1 - t2.astype(F32)
    return t1, t2, r2.astype(BF16)


def _aligned(x, m):
    return x if isinstance(x, int) else pl.multiple_of(x, m)


def _dot(a, b):
    return jnp.dot(a, b, preferred_element_type=F32)


def _dot_nt(a, b):
    return lax.dot_general(a, b, (((1,), (1,)), ((), ())), preferred_element_type=F32)


def _matmul(x, w, gamma=None, residual=None, *, tm, tn):
    m, k = x.shape
    n = w.shape[1]
    prenorm = gamma is not None
    stage = prenorm or x.dtype != BF16
    has_res = residual is not None
    kc = k // W_SPLIT

    def body(*refs):
        it = iter(refs)
        x_ref = next(it)
        g_ref = next(it) if prenorm else None
        w_refs = [next(it) for _ in range(W_SPLIT)]
        r_ref = next(it) if has_res else None
        o_ref = next(it)
        hb_ref = next(it) if stage else None
        if stage:
            @pl.when(pl.program_id(1) == 0)
            def _():
                xv = x_ref[...].astype(F32)
                if prenorm:
                    xv = xv * lax.rsqrt(jnp.mean(xv * xv, axis=-1, keepdims=True) + RMS_EPS) * g_ref[...]
                hb_ref[...] = xv.astype(BF16)
        a_ref = hb_ref if stage else x_ref
        acc = _dot(a_ref[:, 0:kc], w_refs[0][...])
        for c in range(1, W_SPLIT):
            acc = acc + _dot(a_ref[:, c * kc:(c + 1) * kc], w_refs[c][...])
        if has_res:
            acc = acc + r_ref[...]
        o_ref[...] = acc

    in_specs = [pl.BlockSpec((tm, k), lambda i, j: (i, 0))]
    args = [x]
    if prenorm:
        in_specs.append(pl.BlockSpec((1, k), lambda i, j: (0, 0)))
        args.append(gamma.reshape(1, k).astype(F32))
    for c in range(W_SPLIT):
        in_specs.append(pl.BlockSpec((kc, tn), lambda i, j, c=c: (c, j)))
        args.append(w)
    if has_res:
        in_specs.append(pl.BlockSpec((tm, tn), lambda i, j: (i, j)))
        args.append(residual)
    scratch = [pltpu.VMEM((tm, k), BF16)] if stage else []
    vmem = 2 * (tm * k * x.dtype.itemsize + k * tn * 2 + tm * tn * 4 * (2 if has_res else 1)) + tm * k * 2
    return pl.pallas_call(
        body,
        grid=(m // tm, n // tn),
        in_specs=in_specs,
        out_specs=pl.BlockSpec((tm, tn), lambda i, j: (i, j)),
        out_shape=jax.ShapeDtypeStruct((m, n), F32),
        scratch_shapes=scratch,
        compiler_params=_params(("arbitrary", "arbitrary"), vmem // MIB + 8),
    )(*args)


def _ffn(x, wg, wu, wd, tile_expert, n_used, gamma=None, row_gate=None, row_tok=None, *, add_res, tm, tf,
         n_rows=None):
    gather = row_tok is not None
    d = wg.shape[1]
    s = d // LANES
    m = n_rows if gather else x.shape[0]
    ff = wg.shape[2]
    nj = ff // tf
    n_tiles = m // tm
    prenorm = gamma is not None
    use_gate = row_gate is not None
    n_pf = 3 if gather else 2
    dc, fc = d // W_SPLIT, tf // W_SPLIT
    assert not (gather and (add_res or prenorm))

    def body(*refs):
        te_ref, nu_ref = refs[0], refs[1]
        tok_ref = refs[2] if gather else None
        it = iter(refs[n_pf:])
        x_ref = next(it)
        g_ref = next(it) if prenorm else None
        wg_refs = [next(it) for _ in range(W_SPLIT)]
        wu_refs = [next(it) for _ in range(W_SPLIT)]
        wd_refs = [next(it) for _ in range(W_SPLIT)]
        rg_ref = next(it) if use_gate else None
        o_ref = next(it)
        hb_ref = next(it)
        xg_ref, sems = (next(it), next(it)) if gather else (None, None)
        i = pl.program_id(0)
        j = pl.program_id(1)
        active = i < nu_ref[0]

        def row_copy(tile, slot, r):
            tok = tok_ref[tile * tm + r]
            return pltpu.make_async_copy(
                x_ref.at[pl.ds(_aligned(tok * s, s), s), :],
                xg_ref.at[slot, pl.ds(_aligned(r * s, s), s), :],
                sems.at[slot])

        def issue(tile, slot):
            def one(r, carry):
                row_copy(tile, slot, r).start()
                return carry
            lax.fori_loop(0, tm, one, 0, unroll=8)

        def drain(tile, slot):
            def one(r, carry):
                row_copy(tile, slot, r).wait()
                return carry
            lax.fori_loop(0, tm, one, 0, unroll=8)

        @pl.when(jnp.logical_and(active, j == 0))
        def _():
            if gather:
                slot = i % 2

                @pl.when(i == 0)
                def _():
                    issue(0, 0)

                drain(i, slot)
                for c in range(s):
                    hb_ref[:, c * LANES:(c + 1) * LANES] = xg_ref[slot, pl.ds(c, tm, stride=s), :].astype(BF16)

                @pl.when(i + 1 < nu_ref[0])
                def _():
                    issue(i + 1, 1 - slot)
            else:
                xv = x_ref[...]
                if prenorm:
                    xv = xv * lax.rsqrt(jnp.mean(xv * xv, axis=-1, keepdims=True) + RMS_EPS) * g_ref[...]
                hb_ref[...] = xv.astype(BF16)

        @pl.when(j == 0)
        def _():
            o_ref[...] = jnp.zeros_like(o_ref)

        @pl.when(active)
        def _():
            gt = _dot(hb_ref[:, 0:dc], wg_refs[0][0])
            up = _dot(hb_ref[:, 0:dc], wu_refs[0][0])
            for c in range(1, W_SPLIT):
                gt = gt + _dot(hb_ref[:, c * dc:(c + 1) * dc], wg_refs[c][0])
                up = up + _dot(hb_ref[:, c * dc:(c + 1) * dc], wu_refs[c][0])
            a = (gt / (1.0 + jnp.exp(-gt)) * up).astype(BF16)
            y = _dot(a[:, 0:fc], wd_refs[0][0])
            for c in range(1, W_SPLIT):
                y = y + _dot(a[:, c * fc:(c + 1) * fc], wd_refs[c][0])
            o_ref[...] += y

        @pl.when(jnp.logical_and(active, j == nj - 1))
        def _():
            r = o_ref[...]
            if use_gate:
                r = r * rg_ref[...]
            if add_res:
                r = r + x_ref[...]
            o_ref[...] = r

    def last_used(i, pf):
        return jnp.minimum(i, pf[1][0] - 1)

    def col_of(i, j, pf):
        return jnp.where(i < pf[1][0], j, nj - 1)

    if gather:
        in_specs = [pl.BlockSpec(memory_space=pl.ANY)]
    else:
        in_specs = [pl.BlockSpec((tm, d), lambda i, j, *pf: (last_used(i, pf), 0))]
    args = [x]
    if prenorm:
        in_specs.append(pl.BlockSpec((1, d), lambda i, j, *pf: (0, 0)))
        args.append(gamma.reshape(1, d).astype(F32))
    for wmat in (wg, wu):
        for c in range(W_SPLIT):
            in_specs.append(pl.BlockSpec(
                (1, dc, tf), lambda i, j, *pf, c=c: (pf[0][last_used(i, pf)], c, col_of(i, j, pf))))
            args.append(wmat)
    for c in range(W_SPLIT):
        in_specs.append(pl.BlockSpec(
            (1, fc, d), lambda i, j, *pf, c=c: (pf[0][last_used(i, pf)], col_of(i, j, pf) * W_SPLIT + c, 0)))
        args.append(wd)
    if use_gate:
        in_specs.append(pl.BlockSpec((tm, 1), lambda i, j, *pf: (last_used(i, pf), 0)))
        args.append(row_gate.reshape(m, 1))
    scratch = [pltpu.VMEM((tm, d), BF16)]
    vmem = 2 * (3 * d * tf * 2 + tm * d * 4 + (tm * LANES * 4 if use_gate else 0)) + tm * d * 2 + 2 * tm * d * 4
    if gather:
        scratch += [pltpu.VMEM((2, tm * s, LANES), F32), pltpu.SemaphoreType.DMA((2,))]
    prefetch = (tile_expert, n_used) + ((row_tok,) if gather else ())
    return pl.pallas_call(
        body,
        grid_spec=pltpu.PrefetchScalarGridSpec(
            num_scalar_prefetch=n_pf,
            grid=(n_tiles, nj),
            in_specs=in_specs,
            out_specs=pl.BlockSpec((tm, d), lambda i, j, *pf: (i, 0)),
            scratch_shapes=scratch,
        ),
        out_shape=jax.ShapeDtypeStruct((m, d), F32),
        compiler_params=_params(("arbitrary", "arbitrary"), vmem // MIB + 10),
    )(*prefetch, *args)


def _dil_bucket_maps(w, dil):
    i = np.arange(w)[:, None]
    j = np.arange(2 * w)[None, :]
    delta = i + w - j
    valid = (delta >= 0) & (delta <= w)
    bk = np.where(valid, _bucket_np(delta * dil), -1).astype(np.int32)
    first = np.where(j >= w, bk, -1).astype(np.int32)
    return np.stack([first, bk], axis=0)


def _dil_prompt(qkv, table, *, batch, seq, heads, head_dim):
    hd = heads * head_dim
    ng = len(DIL_GROUPS)
    nhp = hd // LANES
    w = DIL_GROUPS[0][0] // DIL_GROUPS[0][1]
    assert all(win // dil == w and seq % (dil * w) == 0 for win, dil in DIL_GROUPS) and 2 * head_dim == LANES
    n_iter = seq // w
    scale = head_dim ** -0.5
    bkt = jnp.asarray(np.stack([_dil_bucket_maps(w, dil) for _, dil in DIL_GROUPS]))
    merge_rows = 256

    def body(table_ref, bkt_ref, *refs):
        q_refs, k_refs, v_refs = refs[0:ng], refs[ng:2 * ng], refs[2 * ng:3 * ng]
        o_ref, bias_ref, og_ref, lg_ref = refs[3 * ng:]
        hp = pl.program_id(0)

        @pl.when(pl.program_id(1) == 0)
        def _():
            for g in range(ng):
                for v in range(2):
                    bk = bkt_ref[g, v]
                    for half in range(2):
                        t = lax.fori_loop(
                            0, N_BUCKETS, lambda k, t: jnp.where(bk == k, table_ref[k, 2 * hp + half], t),
                            jnp.zeros(bk.shape, F32))
                        bias_ref[g, v, half] = jnp.where(bk < 0, NEG_INF, t)

        lane = lax.broadcasted_iota(I32, (w, LANES), 1)
        low = lane < head_dim
        for g, (_, dil) in enumerate(DIL_GROUPS):
            nb = seq // (dil * w)

            def one_block(idx, carry, g=g, dil=dil, nb=nb):
                r = idx // nb
                n = idx - r * nb
                if dil == 1:
                    rows_c = pl.ds(_aligned(n * w, w), w)
                    rows_p = pl.ds(_aligned(jnp.maximum(n - 1, 0) * w, w), w)
                else:
                    rows_c = pl.ds(n * (w * dil) + r, w, stride=dil)
                    rows_p = pl.ds(jnp.maximum(n - 1, 0) * (w * dil) + r, w, stride=dil)
                q2 = q_refs[g][rows_c, :] * scale
                k2 = jnp.concatenate([k_refs[g][rows_p, :], k_refs[g][rows_c, :]], axis=0).astype(BF16)
                v2 = jnp.concatenate([v_refs[g][rows_p, :], v_refs[g][rows_c, :]], axis=0).astype(BF16)
                variant = jnp.minimum(n, 1)
                outs, lses = [], []
                for half in range(2):
                    keep = low if half == 0 else jnp.logical_not(low)
                    qm = jnp.where(keep, q2, 0.0).astype(BF16)
                    sc = _dot_nt(qm, k2) + bias_ref[g, variant, half]
                    mx = jnp.max(sc, axis=-1, keepdims=True)
                    e = jnp.exp(sc - mx)
                    den = jnp.sum(e, axis=-1, keepdims=True)
                    outs.append(_dot(e.astype(BF16), v2) / den)
                    lses.append(mx + jnp.log(den))
                og_ref[g, rows_c, :] = jnp.where(low, outs[0], outs[1])
                lg_ref[g, rows_c, :] = jnp.where(low, lses[0], lses[1])
                return carry

            lax.fori_loop(0, n_iter, one_block, 0, unroll=4)

        def merge(c, carry):
            rows = pl.ds(_aligned(c * merge_rows, merge_rows), merge_rows)
            ls = [lg_ref[g, rows, :] for g in range(ng)]
            mx = functools.reduce(jnp.maximum, ls)
            ws = [jnp.exp(l - mx) for l in ls]
            num = functools.reduce(lambda a, b: a + b, [ws[g] * og_ref[g, rows, :] for g in range(ng)])
            o_ref[rows, :] = (num / functools.reduce(lambda a, b: a + b, ws)).astype(o_ref.dtype)
            return carry

        lax.fori_loop(0, seq // merge_rows, merge, 0)

    def blk(which, g):
        return pl.BlockSpec((seq, LANES), lambda hp, b: (b, (which * ng + g) * nhp + hp))

    in_specs = [pl.BlockSpec(memory_space=pltpu.SMEM),
                pl.BlockSpec((ng, 2, w, 2 * w), lambda hp, b: (0, 0, 0, 0))]
    in_specs += [blk(which, g) for which in range(3) for g in range(ng)]
    return pl.pallas_call(
        body,
        grid=(nhp, batch),
        in_specs=in_specs,
        out_specs=pl.BlockSpec((seq, LANES), lambda hp, b: (b, hp)),
        out_shape=jax.ShapeDtypeStruct((batch * seq, hd), BF16),
        scratch_shapes=[pltpu.VMEM((ng, 2, 2, w, 2 * w), F32), pltpu.VMEM((ng, seq, LANES), F32),
                        pltpu.VMEM((ng, seq, LANES), F32)],
        compiler_params=_params(("arbitrary", "arbitrary"), 48),
    )(table, bkt, *([qkv] * (3 * ng)))


def _dil_sample_group(q4, cache, table, g, *, heads, head_dim, head_chunk, batch_chunk):
    bd = q4.shape[0]
    ng = len(DIL_GROUPS)
    window, dil = DIL_GROUPS[g]
    assert cache.shape[1] == window and heads % head_chunk == 0 and bd % batch_chunk == 0
    kv_t = jnp.transpose(cache, (0, 2, 3, 4, 1))
    w_idx = np.arange(window)
    bias = jnp.where(jnp.asarray(w_idx % dil == 0)[None, :], table[_bucket_np(window - w_idx)].T, NEG_INF)
    bias0 = table[0][:, None]
    n_chunks = heads // head_chunk
    scale = head_dim ** -0.5

    def body(bias_ref, b0_ref, q_ref, *refs):
        k_refs, v_refs = refs[0:head_chunk], refs[head_chunk:2 * head_chunk]
        o_ref, l_ref = refs[2 * head_chunk:]
        h0 = pl.program_id(1) * head_chunk
        row = lax.broadcasted_iota(I32, (head_chunk, head_dim), 0)

        def one_sequence(ib):
            o_acc = jnp.zeros((head_chunk, head_dim), F32)
            l_acc = jnp.zeros((head_chunk, head_dim), F32)
            for hh in range(head_chunk):
                hrow = pl.ds(h0 + hh, 1)
                q = q_ref[ib, g, hrow, :] * scale
                kn, vn = q_ref[ib, ng + g, hrow, :], q_ref[ib, 2 * ng + g, hrow, :]
                kt = k_refs[hh][ib].astype(BF16)
                vt = v_refs[hh][ib].astype(BF16)
                sc = _dot(jnp.broadcast_to(q, (8, head_dim)).astype(BF16), kt)[0:1] + bias_ref[hrow, :]
                s_new = jnp.sum(q * kn, axis=1, keepdims=True) + b0_ref[hrow, :]
                mx = jnp.maximum(jnp.max(sc, axis=1, keepdims=True), s_new)
                e = jnp.exp(sc - mx)
                e_new = jnp.exp(s_new - mx)
                den = jnp.sum(e, axis=1, keepdims=True) + e_new
                pv = _dot_nt(jnp.broadcast_to(e, (8, window)).astype(BF16), vt)[0:1]
                o_acc = jnp.where(row == hh, (pv + e_new * vn) / den, o_acc)
                l_acc = jnp.where(row == hh, mx + jnp.log(den), l_acc)
            o_ref[ib, 0] = o_acc
            l_ref[ib, 0] = l_acc

        if batch_chunk == 1:
            one_sequence(0)
        else:
            def step(ib, carry):
                one_sequence(ib)
                return carry
            lax.fori_loop(0, batch_chunk, step, 0)

    def head_spec(which, hh):
        return pl.BlockSpec((batch_chunk, None, None, head_dim, window),
                            lambda b, hc: (b, which, hc * head_chunk + hh, 0, 0))

    out_spec = pl.BlockSpec((batch_chunk, 1, head_chunk, head_dim), lambda b, hc: (b, hc, 0, 0))
    out_sds = jax.ShapeDtypeStruct((bd, n_chunks, head_chunk, head_dim), F32)
    o, lse = pl.pallas_call(
        body,
        grid=(bd // batch_chunk, n_chunks),
        in_specs=[pl.BlockSpec((heads, window), lambda b, hc: (0, 0)),
                  pl.BlockSpec((heads, 1), lambda b, hc: (0, 0)),
                  pl.BlockSpec((batch_chunk, 3 * ng, heads, head_dim), lambda b, hc: (b, 0, 0, 0))]
        + [head_spec(0, hh) for hh in range(head_chunk)] + [head_spec(1, hh) for hh in range(head_chunk)],
        out_specs=[out_spec, out_spec],
        out_shape=[out_sds, out_sds],
        compiler_params=_params(("arbitrary", "arbitrary"), 32),
    )(bias, bias0, q4, *([kv_t] * (2 * head_chunk)))
    return o.reshape(bd, heads, head_dim), lse.reshape(bd, heads, head_dim)


def _dil_sample(qkv_s, caches, table, *, heads, head_dim):
    bd = qkv_s.shape[0]
    ng = len(DIL_GROUPS)
    q4 = qkv_s.reshape(bd, 3 * ng, heads, head_dim)
    outs, lses = [], []
    for g, c in enumerate(caches):
        head_bytes = 2 * head_dim * DIL_GROUPS[g][0] * 4
        chunk = max(1, min(heads, DIL_SAMPLE_STEP_BYTES // head_bytes))
        seqs = max(1, DIL_SAMPLE_STEP_BYTES // (head_bytes * chunk))
        while bd % seqs:
            seqs -= 1
        o, lse = _dil_sample_group(q4, c, table, g, heads=heads, head_dim=head_dim, head_chunk=chunk,
                                   batch_chunk=seqs)
        outs.append(o)
        lses.append(lse)

    def body(*refs):
        o_refs, l_refs, out_ref = refs[:ng], refs[ng:2 * ng], refs[2 * ng]
        ls = [r[...] for r in l_refs]
        mx = functools.reduce(jnp.maximum, ls)
        ws = [jnp.exp(l - mx) for l in ls]
        num = functools.reduce(lambda a, b: a + b, [wt * r[...] for wt, r in zip(ws, o_refs)])
        out_ref[...] = num / functools.reduce(lambda a, b: a + b, ws)

    spec = pl.BlockSpec((bd, heads, head_dim), lambda i: (0, 0, 0))
    out = pl.pallas_call(
        body,
        grid=(1,),
        in_specs=[spec] * (2 * ng),
        out_specs=spec,
        out_shape=jax.ShapeDtypeStruct((bd, heads, head_dim), F32),
        compiler_params=_params(("arbitrary",), 32),
    )(*outs, *lses)
    return out.reshape(bd, heads * head_dim)


def _block_rank_select(gates, n_sel):
    sels = []
    nblk = len(gates)
    for k in range(nblk):
        rank = jnp.zeros(gates[k].shape, F32)
        for m in range(nblk):
            if m == k:
                continue
            beats = (gates[m] >= gates[k]) if m < k else (gates[m] > gates[k])
            rank = rank + beats.astype(F32)
        sels.append(rank < n_sel)
    return sels


def _moba_prompt(qkv, table, *, batch, seq, heads, kv_heads):
    d = B_HEAD_DIM
    blk = MOBA_BLOCK
    nblk = seq // blk
    grp = heads // kv_heads
    n_sel = min(MOBA_TOPK, nblk)
    scale = d ** -0.5
    r = np.arange(blk)[:, None]
    c = np.arange(blk)[None, :]
    bkt = jnp.asarray(np.stack([_bucket_np(db * blk + r - c) for db in range(nblk)]))

    def body(table_ref, bkt_ref, q_ref, k_ref, v_ref, o_ref, bias_ref, ka_ref, va_ref):
        h = pl.program_id(0)

        @pl.when(pl.program_id(1) == 0)
        def _():
            row = lax.broadcasted_iota(I32, (blk, blk), 0)
            col = lax.broadcasted_iota(I32, (blk, blk), 1)
            for db in range(nblk):
                bk = bkt_ref[db]
                t = lax.fori_loop(0, N_BUCKETS, lambda k, t: jnp.where(bk == k, table_ref[k, h], t),
                                  jnp.zeros((blk, blk), F32))
                if db == 0:
                    t = jnp.where(col > row, NEG_INF, t)
                bias_ref[db] = t

        lane = lax.broadcasted_iota(I32, (blk, d), 1)
        row16 = lax.broadcasted_iota(I32, (16, d), 0)
        kmean = jnp.zeros((16, d), F32)
        for n in range(nblk):
            rows = slice(n * blk, (n + 1) * blk)
            kn = k_ref[rows, :]
            ka_ref[rows, 0:d] = kn.astype(BF16)
            ka_ref[rows, d:2 * d] = jnp.where(lane == n, 1.0, 0.0).astype(BF16)
            va_ref[rows, 0:d] = v_ref[rows, :].astype(BF16)
            va_ref[rows, d:2 * d] = jnp.where(lane == 0, 1.0, 0.0).astype(BF16)
            kmean = jnp.where(row16 == n, jnp.mean(kn, axis=0, keepdims=True), kmean)
        k1, k2, k3 = _split3(kmean)
        row128 = lax.broadcasted_iota(I32, (LANES, blk), 0)

        for i in range(nblk):
            q = q_ref[i * blk:(i + 1) * blk, :]
            qa = (q * scale).astype(BF16)
            ranked = i > n_sel
            if ranked:
                q1, q2, q3 = _split3(q)
                gate_t = (_dot_nt(k1, q1) + _dot_nt(k1, q2) + _dot_nt(k2, q1)
                          + _dot_nt(k2, q2) + _dot_nt(k1, q3) + _dot_nt(k3, q1))
                sels = _block_rank_select([gate_t[n:n + 1, :] for n in range(i)], n_sel)
                sel_t = jnp.where(row128 == i, 1.0, 0.0)
                for n in range(i):
                    sel_t = jnp.where((row128 == n) & sels[n], 1.0, sel_t)
                sel_neg = jnp.where(sel_t.T > 0.5, 0.0, NEG_INF).astype(BF16)
                qa = jnp.concatenate([qa, sel_neg], axis=1)
            acc = None
            mx = None
            for n in range(i + 1):
                rows = slice(n * blk, (n + 1) * blk)
                kk = ka_ref[rows, :] if ranked else ka_ref[rows, 0:d]
                sc = _dot_nt(qa, kk) + bias_ref[i - n]
                blk_max = jnp.max(sc, axis=-1, keepdims=True)
                if n == 0:
                    mx = blk_max
                    acc = _dot(jnp.exp(sc - mx).astype(BF16), va_ref[rows, :])
                else:
                    m_new = jnp.maximum(mx, blk_max)
                    acc = jnp.exp(mx - m_new) * acc + _dot(jnp.exp(sc - m_new).astype(BF16), va_ref[rows, :])
                    mx = m_new
            o_ref[i * blk:(i + 1) * blk, :] = (acc[:, 0:d] / acc[:, d:d + 1]).astype(o_ref.dtype)

    return pl.pallas_call(
        body,
        grid=(heads, batch),
        in_specs=[
            pl.BlockSpec(memory_space=pltpu.SMEM),
            pl.BlockSpec((nblk, blk, blk), lambda h, b: (0, 0, 0)),
            pl.BlockSpec((seq, d), lambda h, b: (b, h)),
            pl.BlockSpec((seq, d), lambda h, b: (b, heads + h // grp)),
            pl.BlockSpec((seq, d), lambda h, b: (b, heads + kv_heads + h // grp)),
        ],
        out_specs=pl.BlockSpec((seq, d), lambda h, b: (b, h)),
        out_shape=jax.ShapeDtypeStruct((batch * seq, heads * d), BF16),
        scratch_shapes=[pltpu.VMEM((nblk, blk, blk), F32), pltpu.VMEM((seq, 2 * d), BF16),
                        pltpu.VMEM((seq, 2 * d), BF16)],
        compiler_params=_params(("arbitrary", "arbitrary"), 40),
    )(table, bkt, qkv, qkv, qkv)


def _moba_sample(q, k_new, v_new, cache, page_table, table, *, heads, kv_heads):
    bd = q.shape[0]
    d = B_HEAD_DIM
    blk = MOBA_BLOCK
    n_pages = page_table.shape[1]
    past = n_pages * PAGE_SIZE
    nblk = past // blk
    rpt = 2 * kv_heads
    assert blk == 2 * PAGE_SIZE and past % blk == 0 and rpt == 8
    grp = heads // kv_heads
    n_sel = min(MOBA_TOPK, nblk + 1)
    scale = d ** -0.5
    page_rows = PAGE_SIZE * rpt
    cols = blk * rpt
    pages = cache.reshape(cache.shape[0], page_rows, d)
    oh = np.zeros((LANES, past), np.float32)
    oh[_bucket_np(past - np.arange(past)), np.arange(past)] = 1.0
    onehot_t = jnp.asarray(oh, BF16)
    ex = np.zeros((blk, cols), np.float32)
    ex[np.arange(cols) // rpt, np.arange(cols)] = 1.0
    expand = jnp.asarray(ex, BF16)
    t_t = jnp.zeros((heads, LANES), F32).at[:, :N_BUCKETS].set(table.astype(F32).T)

    sub = 4 if nblk % 4 == 0 else 1
    psplit = 2
    part_rows = page_rows // psplit
    n_parts = sub * 2 * psplit

    def body(pt_ref, tt_ref, oh_ref, ex_ref, q_ref, kn_ref, vn_ref, *refs):
        part_refs = refs[:n_parts]
        o_ref, bias_ref, m_ref, l_ref, acc_ref, gate_ref = refs[n_parts:]
        b = pl.program_id(0)
        step = pl.program_id(1)

        @pl.when((b == 0) & (step == 0))
        def _():
            t1, t2, t3 = _split3(tt_ref[...])
            full = _dot(t1, oh_ref[...]) + _dot(t2, oh_ref[...]) + _dot(t3, oh_ref[...])
            col = lax.broadcasted_iota(I32, (heads, cols), 1)
            row = lax.broadcasted_iota(I32, (heads, cols), 0)
            own_k_row = (col % rpt) == (row // grp)
            for k in range(nblk):
                f1, f2, f3 = _split3(full[:, k * blk:(k + 1) * blk])
                bias_ref[k] = jnp.where(own_k_row, _dot(f1, ex_ref[...]) + _dot(f2, ex_ref[...])
                                        + _dot(f3, ex_ref[...]), NEG_INF)

        qv = q_ref[0]
        qb = (qv * scale).astype(BF16)
        row_grp = lax.broadcasted_iota(I32, (heads, 1), 0) // grp
        for kb in range(sub):
            n = step * sub + kb
            parts = part_refs[kb * 2 * psplit:(kb + 1) * 2 * psplit]
            x = jnp.concatenate([p[0] for p in parts], axis=0)
            xb = x.astype(BF16)
            sc = _dot_nt(qb, xb) + bias_ref[n]
            mx = jnp.max(sc, axis=1, keepdims=True)
            e = jnp.exp(sc - mx)
            den = jnp.sum(e, axis=1, keepdims=True)
            acc = _dot(pltpu.roll(e, kv_heads, axis=1).astype(BF16), xb)
            ksum = jnp.sum(x.reshape(blk, rpt, d), axis=0)
            gate = jnp.zeros((heads, 1), F32)
            for g in range(kv_heads):
                gg = jnp.sum(qv * (ksum[g:g + 1, :] * (1.0 / blk)), axis=1, keepdims=True)
                gate = jnp.where(row_grp == g, gg, gate)
            m_ref[n] = jnp.broadcast_to(mx, (heads, d))
            l_ref[n] = jnp.broadcast_to(den, (heads, d))
            gate_ref[n] = jnp.broadcast_to(gate, (heads, d))
            acc_ref[n] = acc

        @pl.when(step == nblk // sub - 1)
        def _():
            gates = [gate_ref[k] for k in range(nblk)]
            sels = _block_rank_select(gates, n_sel)
            s_own = jnp.sum(qv * kn_ref[0], axis=1, keepdims=True) * scale + tt_ref[:, 0:1]
            big = jnp.broadcast_to(s_own, (heads, d))
            for k in range(nblk):
                big = jnp.maximum(big, jnp.where(sels[k], m_ref[k], NEG_INF))
            w_own = jnp.exp(s_own - big)
            tot = w_own
            out = w_own * vn_ref[0]
            for k in range(nblk):
                wk = jnp.where(sels[k], jnp.exp(m_ref[k] - big), 0.0)
                tot = tot + wk * l_ref[k]
                out = out + wk * acc_ref[k]
            o_ref[0] = out / tot

    def part_spec(which):
        page, part = which // psplit, which % psplit
        return pl.BlockSpec((1, part_rows, d),
                            lambda b, n, pt: (pt[b * n_pages + 2 * sub * n + page], part, 0))

    row3 = pl.BlockSpec((1, heads, d), lambda b, n, pt: (b, 0, 0))
    stat = pltpu.VMEM((nblk, heads, d), F32)
    return pl.pallas_call(
        body,
        grid_spec=pltpu.PrefetchScalarGridSpec(
            num_scalar_prefetch=1,
            grid=(bd, nblk // sub),
            in_specs=[
                pl.BlockSpec((heads, LANES), lambda b, n, pt: (0, 0)),
                pl.BlockSpec((LANES, past), lambda b, n, pt: (0, 0)),
                pl.BlockSpec((blk, cols), lambda b, n, pt: (0, 0)),
                row3, row3, row3] + [part_spec(which) for which in range(n_parts)],
            out_specs=row3,
            scratch_shapes=[pltpu.VMEM((nblk, heads, cols), F32), stat, stat, stat, stat],
        ),
        out_shape=jax.ShapeDtypeStruct((bd, heads, d), F32),
        compiler_params=_params(("arbitrary", "arbitrary"), 32),
    )(page_table.reshape(-1).astype(I32), t_t, onehot_t, expand, q, k_new, v_new, *([pages] * n_parts))


def _router(x, gamma, w_router, *, tm):
    m, d = x.shape
    ne = w_router.shape[1]
    w_pad = jnp.zeros((d, LANES), F32).at[:, :ne].set(w_router.astype(F32))

    def body(x_ref, g_ref, w_ref, h_ref, idx_ref, gate_ref):
        xv = x_ref[...]
        h = xv * lax.rsqrt(jnp.mean(xv * xv, axis=-1, keepdims=True) + RMS_EPS) * g_ref[...]
        h_ref[...] = h
        h1, h2, h3 = _split3(h)
        w1, w2, w3 = _split3(w_ref[...])
        lg = (_dot(h1, w1) + _dot(h1, w2) + _dot(h2, w1) + _dot(h2, w2) + _dot(h1, w3) + _dot(h3, w1))
        lane = lax.broadcasted_iota(I32, lg.shape, 1)
        lane_f = lane.astype(F32)
        lg = jnp.where(lane < ne, lg, -jnp.inf)
        v1 = jnp.max(lg, axis=-1, keepdims=True)
        i1 = jnp.min(jnp.where(lg == v1, lane_f, float(LANES)), axis=-1, keepdims=True)
        lg2 = jnp.where(lane_f == i1, -jnp.inf, lg)
        v2 = jnp.max(lg2, axis=-1, keepdims=True)
        i2 = jnp.min(jnp.where(lg2 == v2, lane_f, float(LANES)), axis=-1, keepdims=True)
        e2 = jnp.exp(v2 - v1)
        g1 = 1.0 / (1.0 + e2)
        g2 = e2 / (1.0 + e2)
        idx_ref[...] = jnp.where(lane == 0, i1, jnp.where(lane == 1, i2, 0.0)).astype(I32)
        gate_ref[...] = jnp.where(lane == 0, g1, jnp.where(lane == 1, g2, 0.0))

    return pl.pallas_call(
        body,
        grid=(m // tm,),
        in_specs=[pl.BlockSpec((tm, d), lambda i: (i, 0)), pl.BlockSpec((1, d), lambda i: (0, 0)),
                  pl.BlockSpec((d, LANES), lambda i: (0, 0))],
        out_specs=[pl.BlockSpec((tm, d), lambda i: (i, 0)), pl.BlockSpec((tm, LANES), lambda i: (i, 0)),
                   pl.BlockSpec((tm, LANES), lambda i: (i, 0))],
        out_shape=[jax.ShapeDtypeStruct((m, d), F32), jax.ShapeDtypeStruct((m, LANES), I32),
                   jax.ShapeDtypeStruct((m, LANES), F32)],
        compiler_params=_params(("arbitrary",), 40),
    )(x, gamma.reshape(1, d).astype(F32), w_pad)


def _combine(x3, ys3, dest, gamma, *, tc):
    n, s, _ = x3.shape
    d = s * LANES

    def body(dest_ref, x_ref, ys_ref, g_ref, o_ref, buf, sems):
        base = pl.program_id(0) * tc

        def copy(r, k):
            return pltpu.make_async_copy(ys_ref.at[dest_ref[2 * (base + r) + k]], buf.at[k, r], sems.at[k])

        def start(r, carry):
            copy(r, 0).start()
            copy(r, 1).start()
            return carry

        def wait(r, carry):
            copy(r, 0).wait()
            copy(r, 1).wait()
            return carry

        lax.fori_loop(0, tc, start, 0)
        lax.fori_loop(0, tc, wait, 0)
        xv = x_ref[...] + buf[0] + buf[1]
        ms = jnp.sum(jnp.sum(xv * xv, axis=2, keepdims=True), axis=1, keepdims=True) * (1.0 / d)
        o_ref[...] = xv * lax.rsqrt(ms + RMS_EPS) * g_ref[...]

    return pl.pallas_call(
        body,
        grid_spec=pltpu.PrefetchScalarGridSpec(
            num_scalar_prefetch=1,
            grid=(n // tc,),
            in_specs=[pl.BlockSpec((tc, s, LANES), lambda i, dr: (i, 0, 0)),
                      pl.BlockSpec(memory_space=pl.ANY),
                      pl.BlockSpec((1, s, LANES), lambda i, dr: (0, 0, 0))],
            out_specs=pl.BlockSpec((tc, s, LANES), lambda i, dr: (i, 0, 0)),
            scratch_shapes=[pltpu.VMEM((2, tc, s, LANES), F32), pltpu.SemaphoreType.DMA((2,))],
        ),
        out_shape=jax.ShapeDtypeStruct((n, s, LANES), F32),
        compiler_params=_params(("arbitrary",), 6 * tc * d * 4 // MIB + 8),
    )(dest, x3, ys3, gamma.reshape(1, s, LANES).astype(F32))


def _moe(x, gamma_ffn, w_router, wg, wu, wd, gamma_final, *, tm, tf, router_tile, combine_tile):
    n, d = x.shape
    ne = wg.shape[0]
    h, idx, gates = _router(x, gamma_ffn, w_router, tm=router_tile)
    e_flat = idx[:, :TOP_K].reshape(-1)
    onehot = (e_flat[:, None] == jnp.arange(ne, dtype=I32)[None, :]).astype(I32)
    csum = jnp.cumsum(onehot, axis=0)
    rank = jnp.sum((csum - onehot) * onehot, axis=1)
    counts = csum[-1]
    padded = (counts + tm - 1) // tm * tm
    pad_end = jnp.cumsum(padded)
    dest = ((pad_end - padded)[e_flat] + rank).astype(I32)
    n_tiles = -(-(n * TOP_K + ne * (tm - 1)) // tm)
    n_rows = n_tiles * tm
    tile_expert = jnp.minimum(
        jnp.searchsorted(pad_end, jnp.arange(n_tiles, dtype=I32) * tm, side='right'), ne - 1).astype(I32)
    n_used = (pad_end[-1] // tm).astype(I32).reshape(1)
    row_tok = jnp.zeros((n_rows,), I32).at[dest].set(jnp.arange(n * TOP_K, dtype=I32) // TOP_K)
    row_gate = jnp.zeros((n_rows,), F32).at[dest].set(gates[:, :TOP_K].reshape(-1))

    s = d // LANES
    ys = _ffn(h.reshape(n * s, LANES), wg, wu, wd, tile_expert, n_used, row_gate=row_gate, row_tok=row_tok,
              add_res=False, tm=tm, tf=tf, n_rows=n_rows)
    y = _combine(x.reshape(n, s, LANES), ys.reshape(n_rows, s, LANES), dest, gamma_final, tc=combine_tile)
    return y.reshape(n, d)


def kernel(x_prompt, x_sample, cache_win0_kv, cache_win1_kv, cache_win2_kv, cache_moba_kv, page_table, rel_bias, norm_mix, norm_ffn, norm_final, w_qkv_dil, w_o_dil, w_qkv_moba, w_o_moba, w_gate_dense, w_up_dense, w_down_dense, w_router, w_gate_moe, w_up_moe, w_down_moe):
    bp, sp, d = x_prompt.shape
    bd = x_sample.shape[0]
    n_p = bp * sp
    n = n_p + bd
    assert x_sample.shape[1] == 1 and n % ROW_TILE == 0
    table = rel_bias.astype(F32)
    x0 = jnp.concatenate([x_prompt.reshape(n_p, d), x_sample.reshape(bd, d)], axis=0)

    ng = len(DIL_GROUPS)
    qkv_a = _matmul(x0, w_qkv_dil[0].astype(BF16), gamma=norm_mix[0], tm=ROW_TILE, tn=1024)
    o_p = _dil_prompt(qkv_a, table, batch=bp, seq=sp, heads=A_SLOTS, head_dim=A_HEAD_DIM)
    caches = [c[0] for c in (cache_win0_kv, cache_win1_kv, cache_win2_kv)]
    o_s = _dil_sample(qkv_a[n_p:], caches, table, heads=A_SLOTS, head_dim=A_HEAD_DIM)
    o_a = jnp.concatenate([o_p, o_s.astype(BF16)], axis=0)
    x1 = _matmul(o_a, w_o_dil[0].astype(BF16), residual=x0, tm=ROW_TILE, tn=1024)
    dense_tiles = n // ROW_TILE
    x2 = _ffn(x1, w_gate_dense.astype(BF16), w_up_dense.astype(BF16), w_down_dense.astype(BF16),
              jnp.zeros((dense_tiles,), I32), jnp.full((1,), dense_tiles, I32),
              gamma=norm_ffn[0], add_res=True, tm=ROW_TILE, tf=FF_TILE)

    qd, kd = B_HEADS * B_HEAD_DIM, B_KV_HEADS * B_HEAD_DIM
    qkv_b = _matmul(x2, w_qkv_moba[0].astype(BF16), gamma=norm_mix[1], tm=ROW_TILE, tn=1024)
    m_p = _moba_prompt(qkv_b, table, batch=bp, seq=sp, heads=B_HEADS, kv_heads=B_KV_HEADS)
    qkv_bs = qkv_b[n_p:]
    grp = B_HEADS // B_KV_HEADS
    q_s = qkv_bs[:, :qd].reshape(bd, B_HEADS, B_HEAD_DIM)
    k_s = qkv_bs[:, qd:qd + kd].reshape(bd, B_KV_HEADS, B_HEAD_DIM)
    v_s = qkv_bs[:, qd + kd:].reshape(bd, B_KV_HEADS, B_HEAD_DIM)
    m_s = _moba_sample(q_s, jnp.repeat(k_s, grp, axis=1), jnp.repeat(v_s, grp, axis=1), cache_moba_kv[0],
                       page_table, table, heads=B_HEADS, kv_heads=B_KV_HEADS)
    o_b = jnp.concatenate([m_p, m_s.reshape(bd, qd).astype(BF16)], axis=0)
    x3 = _matmul(o_b, w_o_moba[0].astype(BF16), residual=x2, tm=ROW_TILE, tn=1024)
    y = _moe(x3, norm_ffn[1], w_router[0], w_gate_moe[0].astype(BF16), w_up_moe[0].astype(BF16),
             w_down_moe[0].astype(BF16), norm_final,
             tm=MOE_ROW_TILE, tf=FF_TILE, router_tile=ROW_TILE, combine_tile=COMBINE_TILE)

    y_prompt = y[:n_p].reshape(bp, sp, d)
    y_sample = y[n_p:].reshape(bd, 1, d)
    hd = A_SLOTS * A_HEAD_DIM
    qa_s = qkv_a[n_p:].reshape(bd, 1, 3, ng, A_SLOTS, A_HEAD_DIM)
    win_p, win_s = [], []
    for g, (window, _) in enumerate(DIL_GROUPS):
        keep = min(window, sp)
        k_p, v_p = [qkv_a[:n_p, (which * ng + g) * hd:(which * ng + g + 1) * hd]
                    .reshape(bp, sp, A_SLOTS, A_HEAD_DIM)[:, sp - keep:] for which in (1, 2)]
        win_p.append(jnp.stack([k_p, v_p], axis=2)[None])
        win_s.append(jnp.stack([qa_s[:, :, 1, g], qa_s[:, :, 2, g]], axis=2)[None])
    kv_p = qkv_b[:n_p, qd:].reshape(bp, sp // PAGE_SIZE, PAGE_SIZE, 2, B_KV_HEADS, B_HEAD_DIM)[None]
    kv_s = qkv_bs[:, qd:].reshape(bd, 1, 2, B_KV_HEADS, B_HEAD_DIM)[None]
    return (y_prompt, y_sample, win_p[0], win_p[1], win_p[2], win_s[0], win_s[1], win_s[2], kv_p, kv_s)
```

```python
import functools
import math

import numpy as np
import jax
import jax.numpy as jnp
from jax import lax
from jax.experimental import pallas as pl
from jax.experimental.pallas import tpu as pltpu

F32 = jnp.float32
BF16 = jnp.bfloat16
I32 = jnp.int32

RMS_EPS = 1e-6
NEG_INF = -1e30
MIB = 1024 * 1024

N_BUCKETS = 32
MAX_DISTANCE = 2048
DIL_GROUPS = ((128, 1), (512, 4), (2048, 16))
A_SLOTS = 16
A_HEAD_DIM = 64
B_HEADS = 16
B_KV_HEADS = 4
B_HEAD_DIM = 128
MOBA_BLOCK = 256
MOBA_TOPK = 3
PAGE_SIZE = 128
N_EXPERTS = 8
TOP_K = 2

LANES = 128

ROW_TILE = 688
MOE_ROW_TILE = 768
FF_TILE = 512
COMBINE_TILE = 384
W_SPLIT = 1
DIL_SAMPLE_STEP_BYTES = 8 * MIB
DIL_SAMPLE_ROWS = 16


def _params(sem, vmem_mib):
    return pltpu.CompilerParams(dimension_semantics=sem, vmem_limit_bytes=vmem_mib * MIB)


def _bucket_np(dist):
    max_exact = N_BUCKETS // 2
    n = np.maximum(np.asarray(dist, np.int64), 0)
    nf = np.maximum(n, 1).astype(np.float32)
    scale = np.float32((N_BUCKETS - max_exact) / math.log(MAX_DISTANCE / max_exact))
    large = max_exact + (np.log(nf / np.float32(max_exact)) * scale).astype(np.int32)
    return np.where(n < max_exact, n, np.minimum(large, N_BUCKETS - 1)).astype(np.int32)


def _split3(t):
    t1 = t.astype(BF16)
    r1 = t - t1.astype(F32)
    t2 = r1.astype(BF16)
    r2 = r1 - t2.astype(F32)
    return t1, t2, r2.astype(BF16)


def _aligned(x, m):
    return x if isinstance(x, int) else pl.multiple_of(x, m)


def _dot(a, b):
    return jnp.dot(a, b, preferred_element_type=F32)


def _dot_nt(a, b):
    return lax.dot_general(a, b, (((1,), (1,)), ((), ())), preferred_element_type=F32)


def _matmul(x, w, gamma=None, residual=None, *, tm, tn):
    m, k = x.shape
    n = w.shape[1]
    prenorm = gamma is not None
    stage = prenorm or x.dtype != BF16
    has_res = residual is not None
    kc = k // W_SPLIT

    def body(*refs):
        it = iter(refs)
        x_ref = next(it)
        g_ref = next(it) if prenorm else None
        w_refs = [next(it) for _ in range(W_SPLIT)]
        r_ref = next(it) if has_res else None
        o_ref = next(it)
        hb_ref = next(it) if stage else None
        if stage:
            @pl.when(pl.program_id(1) == 0)
            def _():
                xv = x_ref[...].astype(F32)
                if prenorm:
                    xv = xv * lax.rsqrt(jnp.mean(xv * xv, axis=-1, keepdims=True) + RMS_EPS) * g_ref[...]
                hb_ref[...] = xv.astype(BF16)
        a_ref = hb_ref if stage else x_ref
        acc = _dot(a_ref[:, 0:kc], w_refs[0][...])
        for c in range(1, W_SPLIT):
            acc = acc + _dot(a_ref[:, c * kc:(c + 1) * kc], w_refs[c][...])
        if has_res:
            acc = acc + r_ref[...]
        o_ref[...] = acc

    in_specs = [pl.BlockSpec((tm, k), lambda i, j: (i, 0))]
    args = [x]
    if prenorm:
        in_specs.append(pl.BlockSpec((1, k), lambda i, j: (0, 0)))
        args.append(gamma.reshape(1, k).astype(F32))
    for c in range(W_SPLIT):
        in_specs.append(pl.BlockSpec((kc, tn), lambda i, j, c=c: (c, j)))
        args.append(w)
    if has_res:
        in_specs.append(pl.BlockSpec((tm, tn), lambda i, j: (i, j)))
        args.append(residual)
    scratch = [pltpu.VMEM((tm, k), BF16)] if stage else []
    vmem = 2 * (tm * k * x.dtype.itemsize + k * tn * 2 + tm * tn * 4 * (2 if has_res else 1)) + tm * k * 2
    return pl.pallas_call(
        body,
        grid=(m // tm, n // tn),
        in_specs=in_specs,
        out_specs=pl.BlockSpec((tm, tn), lambda i, j: (i, j)),
        out_shape=jax.ShapeDtypeStruct((m, n), F32),
        scratch_shapes=scratch,
        compiler_params=_params(("arbitrary", "arbitrary"), vmem // MIB + 8),
    )(*args)


def _ffn(x, wg, wu, wd, tile_expert, n_used, gamma=None, row_gate=None, row_tok=None, *, add_res, tm, tf,
         n_rows=None):
    gather = row_tok is not None
    d = wg.shape[1]
    s = d // LANES
    m = n_rows if gather else x.shape[0]
    ff = wg.shape[2]
    nj = ff // tf
    n_tiles = m // tm
    prenorm = gamma is not None
    use_gate = row_gate is not None
    n_pf = 3 if gather else 2
    dc, fc = d // W_SPLIT, tf // W_SPLIT
    assert not (gather and (add_res or prenorm))

    def body(*refs):
        te_ref, nu_ref = refs[0], refs[1]
        tok_ref = refs[2] if gather else None
        it = iter(refs[n_pf:])
        x_ref = next(it)
        g_ref = next(it) if prenorm else None
        wg_refs = [next(it) for _ in range(W_SPLIT)]
        wu_refs = [next(it) for _ in range(W_SPLIT)]
        wd_refs = [next(it) for _ in range(W_SPLIT)]
        rg_ref = next(it) if use_gate else None
        o_ref = next(it)
        hb_ref = next(it)
        xg_ref, sems = (next(it), next(it)) if gather else (None, None)
        i = pl.program_id(0)
        j = pl.program_id(1)
        active = i < nu_ref[0]

        def row_copy(tile, slot, r):
            tok = tok_ref[tile * tm + r]
            return pltpu.make_async_copy(
                x_ref.at[pl.ds(_aligned(tok * s, s), s), :],
                xg_ref.at[slot, pl.ds(_aligned(r * s, s), s), :],
                sems.at[slot])

        def issue(tile, slot):
            def one(r, carry):
                row_copy(tile, slot, r).start()
                return carry
            lax.fori_loop(0, tm, one, 0, unroll=8)

        def drain(tile, slot):
            def one(r, carry):
                row_copy(tile, slot, r).wait()
                return carry
            lax.fori_loop(0, tm, one, 0, unroll=8)

        @pl.when(jnp.logical_and(active, j == 0))
        def _():
            if gather:
                slot = i % 2

                @pl.when(i == 0)
                def _():
                    issue(0, 0)

                drain(i, slot)
                for c in range(s):
                    hb_ref[:, c * LANES:(c + 1) * LANES] = xg_ref[slot, pl.ds(c, tm, stride=s), :].astype(BF16)

                @pl.when(i + 1 < nu_ref[0])
                def _():
                    issue(i + 1, 1 - slot)
            else:
                xv = x_ref[...]
                if prenorm:
                    xv = xv * lax.rsqrt(jnp.mean(xv * xv, axis=-1, keepdims=True) + RMS_EPS) * g_ref[...]
                hb_ref[...] = xv.astype(BF16)

        @pl.when(j == 0)
        def _():
            o_ref[...] = jnp.zeros_like(o_ref)

        @pl.when(active)
        def _():
            gt = _dot(hb_ref[:, 0:dc], wg_refs[0][0])
            up = _dot(hb_ref[:, 0:dc], wu_refs[0][0])
            for c in range(1, W_SPLIT):
                gt = gt + _dot(hb_ref[:, c * dc:(c + 1) * dc], wg_refs[c][0])
                up = up + _dot(hb_ref[:, c * dc:(c + 1) * dc], wu_refs[c][0])
            a = (gt / (1.0 + jnp.exp(-gt)) * up).astype(BF16)
            y = _dot(a[:, 0:fc], wd_refs[0][0])
            for c in range(1, W_SPLIT):
                y = y + _dot(a[:, c * fc:(c + 1) * fc], wd_refs[c][0])
            o_ref[...] += y

        @pl.when(jnp.logical_and(active, j == nj - 1))
        def _():
            r = o_ref[...]
            if use_gate:
                r = r * rg_ref[...]
            if add_res:
                r = r + x_ref[...]
            o_ref[...] = r

    def last_used(i, pf):
        return jnp.minimum(i, pf[1][0] - 1)

    def col_of(i, j, pf):
        return jnp.where(i < pf[1][0], j, nj - 1)

    if gather:
        in_specs = [pl.BlockSpec(memory_space=pl.ANY)]
    else:
        in_specs = [pl.BlockSpec((tm, d), lambda i, j, *pf: (last_used(i, pf), 0))]
    args = [x]
    if prenorm:
        in_specs.append(pl.BlockSpec((1, d), lambda i, j, *pf: (0, 0)))
        args.append(gamma.reshape(1, d).astype(F32))
    for wmat in (wg, wu):
        for c in range(W_SPLIT):
            in_specs.append(pl.BlockSpec(
                (1, dc, tf), lambda i, j, *pf, c=c: (pf[0][last_used(i, pf)], c, col_of(i, j, pf))))
            args.append(wmat)
    for c in range(W_SPLIT):
        in_specs.append(pl.BlockSpec(
            (1, fc, d), lambda i, j, *pf, c=c: (pf[0][last_used(i, pf)], col_of(i, j, pf) * W_SPLIT + c, 0)))
        args.append(wd)
    if use_gate:
        in_specs.append(pl.BlockSpec((tm, 1), lambda i, j, *pf: (last_used(i, pf), 0)))
        args.append(row_gate.reshape(m, 1))
    scratch = [pltpu.VMEM((tm, d), BF16)]
    vmem = 2 * (3 * d * tf * 2 + tm * d * 4 + (tm * LANES * 4 if use_gate else 0)) + tm * d * 2 + 2 * tm * d * 4
    if gather:
        scratch += [pltpu.VMEM((2, tm * s, LANES), F32), pltpu.SemaphoreType.DMA((2,))]
    prefetch = (tile_expert, n_used) + ((row_tok,) if gather else ())
    return pl.pallas_call(
        body,
        grid_spec=pltpu.PrefetchScalarGridSpec(
            num_scalar_prefetch=n_pf,
            grid=(n_tiles, nj),
            in_specs=in_specs,
            out_specs=pl.BlockSpec((tm, d), lambda i, j, *pf: (i, 0)),
            scratch_shapes=scratch,
        ),
        out_shape=jax.ShapeDtypeStruct((m, d), F32),
        compiler_params=_params(("arbitrary", "arbitrary"), vmem // MIB + 10),
    )(*prefetch, *args)


def _dil_bucket_maps(w, dil):
    i = np.arange(w)[:, None]
    j = np.arange(2 * w)[None, :]
    delta = i + w - j
    valid = (delta >= 0) & (delta <= w)
    bk = np.where(valid, _bucket_np(delta * dil), -1).astype(np.int32)
    first = np.where(j >= w, bk, -1).astype(np.int32)
    return np.stack([first, bk], axis=0)


def _dil_prompt(qkv, table, *, batch, seq, heads, head_dim):
    hd = heads * head_dim
    ng = len(DIL_GROUPS)
    nhp = hd // LANES
    w = DIL_GROUPS[0][0] // DIL_GROUPS[0][1]
    assert all(win // dil == w and seq % (dil * w) == 0 for win, dil in DIL_GROUPS) and 2 * head_dim == LANES
    n_iter = seq // w
    scale = head_dim ** -0.5
    bkt = jnp.asarray(np.stack([_dil_bucket_maps(w, dil) for _, dil in DIL_GROUPS]))
    merge_rows = 256

    def body(table_ref, bkt_ref, *refs):
        q_refs, k_refs, v_refs = refs[0:ng], refs[ng:2 * ng], refs[2 * ng:3 * ng]
        o_ref, bias_ref, og_ref, lg_ref = refs[3 * ng:]
        hp = pl.program_id(0)

        @pl.when(pl.program_id(1) == 0)
        def _():
            for g in range(ng):
                for v in range(2):
                    bk = bkt_ref[g, v]
                    for half in range(2):
                        t = lax.fori_loop(
                            0, N_BUCKETS, lambda k, t: jnp.where(bk == k, table_ref[k, 2 * hp + half], t),
                            jnp.zeros(bk.shape, F32))
                        bias_ref[g, v, half] = jnp.where(bk < 0, NEG_INF, t)

        lane = lax.broadcasted_iota(I32, (w, LANES), 1)
        low = lane < head_dim
        for g, (_, dil) in enumerate(DIL_GROUPS):
            nb = seq // (dil * w)

            def one_block(idx, carry, g=g, dil=dil, nb=nb):
                r = idx // nb
                n = idx - r * nb
                if dil == 1:
                    rows_c = pl.ds(_aligned(n * w, w), w)
                    rows_p = pl.ds(_aligned(jnp.maximum(n - 1, 0) * w, w), w)
                else:
                    rows_c = pl.ds(n * (w * dil) + r, w, stride=dil)
                    rows_p = pl.ds(jnp.maximum(n - 1, 0) * (w * dil) + r, w, stride=dil)
                q2 = q_refs[g][rows_c, :] * scale
                k2 = jnp.concatenate([k_refs[g][rows_p, :], k_refs[g][rows_c, :]], axis=0).astype(BF16)
                v2 = jnp.concatenate([v_refs[g][rows_p, :], v_refs[g][rows_c, :]], axis=0).astype(BF16)
                variant = jnp.minimum(n, 1)
                outs, lses = [], []
                for half in range(2):
                    keep = low if half == 0 else jnp.logical_not(low)
                    qm = jnp.where(keep, q2, 0.0).astype(BF16)
                    sc = _dot_nt(qm, k2) + bias_ref[g, variant, half]
                    mx = jnp.max(sc, axis=-1, keepdims=True)
                    e = jnp.exp(sc - mx)
                    den = jnp.sum(e, axis=-1, keepdims=True)
                    outs.append(_dot(e.astype(BF16), v2) / den)
                    lses.append(mx + jnp.log(den))
                og_ref[g, rows_c, :] = jnp.where(low, outs[0], outs[1])
                lg_ref[g, rows_c, :] = jnp.where(low, lses[0], lses[1])
                return carry

            lax.fori_loop(0, n_iter, one_block, 0, unroll=4)

        def merge(c, carry):
            rows = pl.ds(_aligned(c * merge_rows, merge_rows), merge_rows)
            ls = [lg_ref[g, rows, :] for g in range(ng)]
            mx = functools.reduce(jnp.maximum, ls)
            ws = [jnp.exp(l - mx) for l in ls]
            num = functools.reduce(lambda a, b: a + b, [ws[g] * og_ref[g, rows, :] for g in range(ng)])
            o_ref[rows, :] = (num / functools.reduce(lambda a, b: a + b, ws)).astype(o_ref.dtype)
            return carry

        lax.fori_loop(0, seq // merge_rows, merge, 0)

    def blk(which, g):
        return pl.BlockSpec((seq, LANES), lambda hp, b: (b, (which * ng + g) * nhp + hp))

    in_specs = [pl.BlockSpec(memory_space=pltpu.SMEM),
                pl.BlockSpec((ng, 2, w, 2 * w), lambda hp, b: (0, 0, 0, 0))]
    in_specs += [blk(which, g) for which in range(3) for g in range(ng)]
    return pl.pallas_call(
        body,
        grid=(nhp, batch),
        in_specs=in_specs,
        out_specs=pl.BlockSpec((seq, LANES), lambda hp, b: (b, hp)),
        out_shape=jax.ShapeDtypeStruct((batch * seq, hd), BF16),
        scratch_shapes=[pltpu.VMEM((ng, 2, 2, w, 2 * w), F32), pltpu.VMEM((ng, seq, LANES), F32),
                        pltpu.VMEM((ng, seq, LANES), F32)],
        compiler_params=_params(("arbitrary", "arbitrary"), 48),
    )(table, bkt, *([qkv] * (3 * ng)))


def _dil_sample_group(qkv3, cache, table, g, *, heads, head_dim, head_chunk, batch_chunk):
    bd = qkv3.shape[0]
    ng = len(DIL_GROUPS)
    hd = heads * head_dim
    window, dil = DIL_GROUPS[g]
    rows = head_chunk * head_dim
    mr = DIL_SAMPLE_ROWS
    parts = 4
    assert cache.shape[1] == window and heads % head_chunk == 0 and bd % batch_chunk == 0
    assert head_chunk in (8, 16) and rows % (parts * 8) == 0 and rows % LANES == 0
    kv_t = jnp.transpose(cache, (0, 2, 3, 4, 1)).reshape(bd, 2, hd, window)
    w_idx = np.arange(window)
    bias = jnp.where(jnp.asarray(w_idx % dil == 0)[None, :], table[_bucket_np(window - w_idx)].T, NEG_INF)
    bias0 = table[0][:, None]
    n_chunks = heads // head_chunk
    scale = head_dim ** -0.5

    def body(bias_ref, b0_ref, q_ref, kn_ref, vn_ref, *refs):
        k_refs, v_refs = refs[0:parts], refs[parts:2 * parts]
        o_ref, l_ref = refs[2 * parts:]
        h0 = _aligned(pl.program_id(1) * head_chunk, head_chunk)
        own = (lax.broadcasted_iota(I32, (mr, rows), 1) // head_dim) == lax.broadcasted_iota(I32, (mr, rows), 0)
        own_c = own[0:head_chunk]
        bias_c = bias_ref[pl.ds(h0, head_chunk), :]
        bias0_c = b0_ref[pl.ds(h0, head_chunk), :]

        def one_sequence(ib):
            q = q_ref[ib] * scale
            kt = jnp.concatenate([r[ib] for r in k_refs], axis=0).astype(BF16)
            vt = jnp.concatenate([r[ib] for r in v_refs], axis=0).astype(BF16)
            q_bd = jnp.where(own, jnp.broadcast_to(q, (mr, rows)), 0.0).astype(BF16)
            sc = _dot(q_bd, kt)[0:head_chunk] + bias_c
            s_new = jnp.sum(jnp.where(own_c, q * kn_ref[ib], 0.0), axis=1, keepdims=True) + bias0_c
            mx = jnp.maximum(jnp.max(sc, axis=1, keepdims=True), s_new)
            e = jnp.exp(sc - mx)
            e_new = jnp.exp(s_new - mx)
            den = jnp.sum(e, axis=1, keepdims=True) + e_new
            if head_chunk < mr:
                e = jnp.concatenate([e, jnp.zeros((mr - head_chunk, window), F32)], axis=0)
            pv = _dot_nt(e.astype(BF16), vt)[0:head_chunk]
            o_rows = jnp.where(own_c, (pv + e_new * vn_ref[ib]) / den, 0.0)
            o_ref[ib] = jnp.sum(o_rows, axis=0, keepdims=True)
            l_ref[ib] = jnp.sum(jnp.where(own_c, mx + jnp.log(den), 0.0), axis=0, keepdims=True)

        if batch_chunk == 1:
            one_sequence(0)
        else:
            def step(ib, carry):
                one_sequence(ib)
                return carry
            lax.fori_loop(0, batch_chunk, step, 0)

    def part_spec(which, p):
        return pl.BlockSpec((batch_chunk, None, rows // parts, window),
                            lambda b, hc: (b, which, hc * parts + p, 0))

    def row_spec(which):
        return pl.BlockSpec((batch_chunk, 1, rows), lambda b, hc: (b, 0, (which * ng + g) * n_chunks + hc))

    out_spec = pl.BlockSpec((batch_chunk, 1, rows), lambda b, hc: (b, 0, hc))
    out_sds = jax.ShapeDtypeStruct((bd, 1, hd), F32)
    return pl.pallas_call(
        body,
        grid=(bd // batch_chunk, n_chunks),
        in_specs=[pl.BlockSpec((heads, window), lambda b, hc: (0, 0)),
                  pl.BlockSpec((heads, 1), lambda b, hc: (0, 0)),
                  row_spec(0), row_spec(1), row_spec(2)]
        + [part_spec(0, p) for p in range(parts)] + [part_spec(1, p) for p in range(parts)],
        out_specs=[out_spec, out_spec],
        out_shape=[out_sds, out_sds],
        compiler_params=_params(("arbitrary", "arbitrary"), 40),
    )(bias, bias0, qkv3, qkv3, qkv3, *([kv_t] * (2 * parts)))


def _dil_sample(qkv_s, caches, table, *, heads, head_dim):
    bd = qkv_s.shape[0]
    ng = len(DIL_GROUPS)
    hd = heads * head_dim
    qkv3 = qkv_s.reshape(bd, 1, 3 * ng * hd)
    outs, lses = [], []
    for g, c in enumerate(caches):
        head_bytes = 2 * head_dim * DIL_GROUPS[g][0] * 4
        chunk = 16 if 16 * head_bytes <= DIL_SAMPLE_STEP_BYTES else 8
        seqs = max(1, DIL_SAMPLE_STEP_BYTES // (head_bytes * chunk))
        while bd % seqs:
            seqs -= 1
        o, lse = _dil_sample_group(qkv3, c, table, g, heads=heads, head_dim=head_dim, head_chunk=chunk,
                                   batch_chunk=seqs)
        outs.append(o.reshape(bd, hd))
        lses.append(lse.reshape(bd, hd))

    def body(*refs):
        o_refs, l_refs, out_ref = refs[:ng], refs[ng:2 * ng], refs[2 * ng]
        ls = [r[...] for r in l_refs]
        mx = functools.reduce(jnp.maximum, ls)
        ws = [jnp.exp(l - mx) for l in ls]
        num = functools.reduce(lambda a, b: a + b, [wt * r[...] for wt, r in zip(ws, o_refs)])
        out_ref[...] = num / functools.reduce(lambda a, b: a + b, ws)

    spec = pl.BlockSpec((bd, hd), lambda i: (0, 0))
    return pl.pallas_call(
        body,
        grid=(1,),
        in_specs=[spec] * (2 * ng),
        out_specs=spec,
        out_shape=jax.ShapeDtypeStruct((bd, hd), F32),
        compiler_params=_params(("arbitrary",), 32),
    )(*outs, *lses)


def _block_rank_select(gates, n_sel):
    sels = []
    nblk = len(gates)
    for k in range(nblk):
        rank = jnp.zeros(gates[k].shape, F32)
        for m in range(nblk):
            if m == k:
                continue
            beats = (gates[m] >= gates[k]) if m < k else (gates[m] > gates[k])
            rank = rank + beats.astype(F32)
        sels.append(rank < n_sel)
    return sels


def _moba_prompt(qkv, table, *, batch, seq, heads, kv_heads):
    d = B_HEAD_DIM
    blk = MOBA_BLOCK
    nblk = seq // blk
    grp = heads // kv_heads
    n_sel = min(MOBA_TOPK, nblk)
    scale = d ** -0.5
    r = np.arange(blk)[:, None]
    c = np.arange(blk)[None, :]
    bkt = jnp.asarray(np.stack([_bucket_np(db * blk + r - c) for db in range(nblk)]))

    def body(table_ref, bkt_ref, q_ref, k_ref, v_ref, o_ref, bias_ref, ka_ref, va_ref):
        h = pl.program_id(0)

        @pl.when(pl.program_id(1) == 0)
        def _():
            row = lax.broadcasted_iota(I32, (blk, blk), 0)
            col = lax.broadcasted_iota(I32, (blk, blk), 1)
            for db in range(nblk):
                bk = bkt_ref[db]
                t = lax.fori_loop(0, N_BUCKETS, lambda k, t: jnp.where(bk == k, table_ref[k, h], t),
                                  jnp.zeros((blk, blk), F32))
                if db == 0:
                    t = jnp.where(col > row, NEG_INF, t)
                bias_ref[db] = t

        lane = lax.broadcasted_iota(I32, (blk, d), 1)
        row16 = lax.broadcasted_iota(I32, (16, d), 0)
        kmean = jnp.zeros((16, d), F32)
        for n in range(nblk):
            rows = slice(n * blk, (n + 1) * blk)
            kn = k_ref[rows, :]
            ka_ref[rows, 0:d] = kn.astype(BF16)
            ka_ref[rows, d:2 * d] = jnp.where(lane == n, 1.0, 0.0).astype(BF16)
            va_ref[rows, 0:d] = v_ref[rows, :].astype(BF16)
            va_ref[rows, d:2 * d] = jnp.where(lane == 0, 1.0, 0.0).astype(BF16)
            kmean = jnp.where(row16 == n, jnp.mean(kn, axis=0, keepdims=True), kmean)
        k1, k2, k3 = _split3(kmean)
        row128 = lax.broadcasted_iota(I32, (LANES, blk), 0)

        for i in range(nblk):
            q = q_ref[i * blk:(i + 1) * blk, :]
            qa = (q * scale).astype(BF16)
            ranked = i > n_sel
            if ranked:
                q1, q2, q3 = _split3(q)
                gate_t = (_dot_nt(k1, q1) + _dot_nt(k1, q2) + _dot_nt(k2, q1)
                          + _dot_nt(k2, q2) + _dot_nt(k1, q3) + _dot_nt(k3, q1))
                sels = _block_rank_select([gate_t[n:n + 1, :] for n in range(i)], n_sel)
                sel_t = jnp.where(row128 == i, 1.0, 0.0)
                for n in range(i):
                    sel_t = jnp.where((row128 == n) & sels[n], 1.0, sel_t)
                sel_neg = jnp.where(sel_t.T > 0.5, 0.0, NEG_INF).astype(BF16)
                qa = jnp.concatenate([qa, sel_neg], axis=1)
            acc = None
            mx = None
            for n in range(i + 1):
                rows = slice(n * blk, (n + 1) * blk)
                kk = ka_ref[rows, :] if ranked else ka_ref[rows, 0:d]
                sc = _dot_nt(qa, kk) + bias_ref[i - n]
                blk_max = jnp.max(sc, axis=-1, keepdims=True)
                if n == 0:
                    mx = blk_max
                    acc = _dot(jnp.exp(sc - mx).astype(BF16), va_ref[rows, :])
                else:
                    m_new = jnp.maximum(mx, blk_max)
                    acc = jnp.exp(mx - m_new) * acc + _dot(jnp.exp(sc - m_new).astype(BF16), va_ref[rows, :])
                    mx = m_new
            o_ref[i * blk:(i + 1) * blk, :] = (acc[:, 0:d] / acc[:, d:d + 1]).astype(o_ref.dtype)

    return pl.pallas_call(
        body,
        grid=(heads, batch),
        in_specs=[
            pl.BlockSpec(memory_space=pltpu.SMEM),
            pl.BlockSpec((nblk, blk, blk), lambda h, b: (0, 0, 0)),
            pl.BlockSpec((seq, d), lambda h, b: (b, h)),
            pl.BlockSpec((seq, d), lambda h, b: (b, heads + h // grp)),
            pl.BlockSpec((seq, d), lambda h, b: (b, heads + kv_heads + h // grp)),
        ],
        out_specs=pl.BlockSpec((seq, d), lambda h, b: (b, h)),
        out_shape=jax.ShapeDtypeStruct((batch * seq, heads * d), BF16),
        scratch_shapes=[pltpu.VMEM((nblk, blk, blk), F32), pltpu.VMEM((seq, 2 * d), BF16),
                        pltpu.VMEM((seq, 2 * d), BF16)],
        compiler_params=_params(("arbitrary", "arbitrary"), 40),
    )(table, bkt, qkv, qkv, qkv)


def _moba_sample(q, k_new, v_new, cache, page_table, table, *, heads, kv_heads):
    bd = q.shape[0]
    d = B_HEAD_DIM
    blk = MOBA_BLOCK
    n_pages = page_table.shape[1]
    past = n_pages * PAGE_SIZE
    nblk = past // blk
    rpt = 2 * kv_heads
    assert blk == 2 * PAGE_SIZE and past % blk == 0 and rpt == 8
    grp = heads // kv_heads
    n_sel = min(MOBA_TOPK, nblk + 1)
    scale = d ** -0.5
    page_rows = PAGE_SIZE * rpt
    cols = blk * rpt
    pages = cache.reshape(cache.shape[0], page_rows, d)
    oh = np.zeros((LANES, past), np.float32)
    oh[_bucket_np(past - np.arange(past)), np.arange(past)] = 1.0
    onehot_t = jnp.asarray(oh, BF16)
    ex = np.zeros((blk, cols), np.float32)
    ex[np.arange(cols) // rpt, np.arange(cols)] = 1.0
    expand = jnp.asarray(ex, BF16)
    t_t = jnp.zeros((heads, LANES), F32).at[:, :N_BUCKETS].set(table.astype(F32).T)

    sub = 4 if nblk % 4 == 0 else 1
    psplit = 2
    part_rows = page_rows // psplit
    n_parts = sub * 2 * psplit

    def body(pt_ref, tt_ref, oh_ref, ex_ref, q_ref, kn_ref, vn_ref, *refs):
        part_refs = refs[:n_parts]
        o_ref, bias_ref, m_ref, l_ref, acc_ref, gate_ref = refs[n_parts:]
        b = pl.program_id(0)
        step = pl.program_id(1)

        @pl.when((b == 0) & (step == 0))
        def _():
            t1, t2, t3 = _split3(tt_ref[...])
            full = _dot(t1, oh_ref[...]) + _dot(t2, oh_ref[...]) + _dot(t3, oh_ref[...])
            col = lax.broadcasted_iota(I32, (heads, cols), 1)
            row = lax.broadcasted_iota(I32, (heads, cols), 0)
            own_k_row = (col % rpt) == (row // grp)
            for k in range(nblk):
                f1, f2, f3 = _split3(full[:, k * blk:(k + 1) * blk])
                bias_ref[k] = jnp.where(own_k_row, _dot(f1, ex_ref[...]) + _dot(f2, ex_ref[...])
                                        + _dot(f3, ex_ref[...]), NEG_INF)

        qv = q_ref[0]
        qb = (qv * scale).astype(BF16)
        row_grp = lax.broadcasted_iota(I32, (heads, 1), 0) // grp
        for kb in range(sub):
            n = step * sub + kb
            parts = part_refs[kb * 2 * psplit:(kb + 1) * 2 * psplit]
            x = jnp.concatenate([p[0] for p in parts], axis=0)
            xb = x.astype(BF16)
            sc = _dot_nt(qb, xb) + bias_ref[n]
            mx = jnp.max(sc, axis=1, keepdims=True)
            e = jnp.exp(sc - mx)
            den = jnp.sum(e, axis=1, keepdims=True)
            acc = _dot(pltpu.roll(e, kv_heads, axis=1).astype(BF16), xb)
            ksum = jnp.sum(x.reshape(blk, rpt, d), axis=0)
            gate = jnp.zeros((heads, 1), F32)
            for g in range(kv_heads):
                gg = jnp.sum(qv * (ksum[g:g + 1, :] * (1.0 / blk)), axis=1, keepdims=True)
                gate = jnp.where(row_grp == g, gg, gate)
            m_ref[n] = jnp.broadcast_to(mx, (heads, d))
            l_ref[n] = jnp.broadcast_to(den, (heads, d))
            gate_ref[n] = jnp.broadcast_to(gate, (heads, d))
            acc_ref[n] = acc

        @pl.when(step == nblk // sub - 1)
        def _():
            gates = [gate_ref[k] for k in range(nblk)]
            sels = _block_rank_select(gates, n_sel)
            s_own = jnp.sum(qv * kn_ref[0], axis=1, keepdims=True) * scale + tt_ref[:, 0:1]
            big = jnp.broadcast_to(s_own, (heads, d))
            for k in range(nblk):
                big = jnp.maximum(big, jnp.where(sels[k], m_ref[k], NEG_INF))
            w_own = jnp.exp(s_own - big)
            tot = w_own
            out = w_own * vn_ref[0]
            for k in range(nblk):
                wk = jnp.where(sels[k], jnp.exp(m_ref[k] - big), 0.0)
                tot = tot + wk * l_ref[k]
                out = out + wk * acc_ref[k]
            o_ref[0] = out / tot

    def part_spec(which):
        page, part = which // psplit, which % psplit
        return pl.BlockSpec((1, part_rows, d),
                            lambda b, n, pt: (pt[b * n_pages + 2 * sub * n + page], part, 0))

    row3 = pl.BlockSpec((1, heads, d), lambda b, n, pt: (b, 0, 0))
    stat = pltpu.VMEM((nblk, heads, d), F32)
    return pl.pallas_call(
        body,
        grid_spec=pltpu.PrefetchScalarGridSpec(
            num_scalar_prefetch=1,
            grid=(bd, nblk // sub),
            in_specs=[
                pl.BlockSpec((heads, LANES), lambda b, n, pt: (0, 0)),
                pl.BlockSpec((LANES, past), lambda b, n, pt: (0, 0)),
                pl.BlockSpec((blk, cols), lambda b, n, pt: (0, 0)),
                row3, row3, row3] + [part_spec(which) for which in range(n_parts)],
            out_specs=row3,
            scratch_shapes=[pltpu.VMEM((nblk, heads, cols), F32), stat, stat, stat, stat],
        ),
        out_shape=jax.ShapeDtypeStruct((bd, heads, d), F32),
        compiler_params=_params(("arbitrary", "arbitrary"), 32),
    )(page_table.reshape(-1).astype(I32), t_t, onehot_t, expand, q, k_new, v_new, *([pages] * n_parts))


def _router(x, gamma, w_router, *, tm):
    m, d = x.shape
    ne = w_router.shape[1]
    w_pad = jnp.zeros((d, LANES), F32).at[:, :ne].set(w_router.astype(F32))

    def body(x_ref, g_ref, w_ref, h_ref, idx_ref, gate_ref):
        xv = x_ref[...]
        h = xv * lax.rsqrt(jnp.mean(xv * xv, axis=-1, keepdims=True) + RMS_EPS) * g_ref[...]
        h_ref[...] = h
        h1, h2, h3 = _split3(h)
        w1, w2, w3 = _split3(w_ref[...])
        lg = (_dot(h1, w1) + _dot(h1, w2) + _dot(h2, w1) + _dot(h2, w2) + _dot(h1, w3) + _dot(h3, w1))
        lane = lax.broadcasted_iota(I32, lg.shape, 1)
        lane_f = lane.astype(F32)
        lg = jnp.where(lane < ne, lg, -jnp.inf)
        v1 = jnp.max(lg, axis=-1, keepdims=True)
        i1 = jnp.min(jnp.where(lg == v1, lane_f, float(LANES)), axis=-1, keepdims=True)
        lg2 = jnp.where(lane_f == i1, -jnp.inf, lg)
        v2 = jnp.max(lg2, axis=-1, keepdims=True)
        i2 = jnp.min(jnp.where(lg2 == v2, lane_f, float(LANES)), axis=-1, keepdims=True)
        e2 = jnp.exp(v2 - v1)
        g1 = 1.0 / (1.0 + e2)
        g2 = e2 / (1.0 + e2)
        idx_ref[...] = jnp.where(lane == 0, i1, jnp.where(lane == 1, i2, 0.0)).astype(I32)
        gate_ref[...] = jnp.where(lane == 0, g1, jnp.where(lane == 1, g2, 0.0))

    return pl.pallas_call(
        body,
        grid=(m // tm,),
        in_specs=[pl.BlockSpec((tm, d), lambda i: (i, 0)), pl.BlockSpec((1, d), lambda i: (0, 0)),
                  pl.BlockSpec((d, LANES), lambda i: (0, 0))],
        out_specs=[pl.BlockSpec((tm, d), lambda i: (i, 0)), pl.BlockSpec((tm, LANES), lambda i: (i, 0)),
                   pl.BlockSpec((tm, LANES), lambda i: (i, 0))],
        out_shape=[jax.ShapeDtypeStruct((m, d), F32), jax.ShapeDtypeStruct((m, LANES), I32),
                   jax.ShapeDtypeStruct((m, LANES), F32)],
        compiler_params=_params(("arbitrary",), 40),
    )(x, gamma.reshape(1, d).astype(F32), w_pad)


def _combine(x3, ys3, dest, gamma, *, tc):
    n, s, _ = x3.shape
    d = s * LANES

    def body(dest_ref, x_ref, ys_ref, g_ref, o_ref, buf, sems):
        base = pl.program_id(0) * tc

        def copy(r, k):
            return pltpu.make_async_copy(ys_ref.at[dest_ref[2 * (base + r) + k]], buf.at[k, r], sems.at[k])

        def start(r, carry):
            copy(r, 0).start()
            copy(r, 1).start()
            return carry

        def wait(r, carry):
            copy(r, 0).wait()
            copy(r, 1).wait()
            return carry

        lax.fori_loop(0, tc, start, 0)
        lax.fori_loop(0, tc, wait, 0)
        xv = x_ref[...] + buf[0] + buf[1]
        ms = jnp.sum(jnp.sum(xv * xv, axis=2, keepdims=True), axis=1, keepdims=True) * (1.0 / d)
        o_ref[...] = xv * lax.rsqrt(ms + RMS_EPS) * g_ref[...]

    return pl.pallas_call(
        body,
        grid_spec=pltpu.PrefetchScalarGridSpec(
            num_scalar_prefetch=1,
            grid=(n // tc,),
            in_specs=[pl.BlockSpec((tc, s, LANES), lambda i, dr: (i, 0, 0)),
                      pl.BlockSpec(memory_space=pl.ANY),
                      pl.BlockSpec((1, s, LANES), lambda i, dr: (0, 0, 0))],
            out_specs=pl.BlockSpec((tc, s, LANES), lambda i, dr: (i, 0, 0)),
            scratch_shapes=[pltpu.VMEM((2, tc, s, LANES), F32), pltpu.SemaphoreType.DMA((2,))],
        ),
        out_shape=jax.ShapeDtypeStruct((n, s, LANES), F32),
        compiler_params=_params(("arbitrary",), 6 * tc * d * 4 // MIB + 8),
    )(dest, x3, ys3, gamma.reshape(1, s, LANES).astype(F32))


def _moe(x, gamma_ffn, w_router, wg, wu, wd, gamma_final, *, tm, tf, router_tile, combine_tile):
    n, d = x.shape
    ne = wg.shape[0]
    h, idx, gates = _router(x, gamma_ffn, w_router, tm=router_tile)
    e_flat = idx[:, :TOP_K].reshape(-1)
    onehot = (e_flat[:, None] == jnp.arange(ne, dtype=I32)[None, :]).astype(I32)
    csum = jnp.cumsum(onehot, axis=0)
    rank = jnp.sum((csum - onehot) * onehot, axis=1)
    counts = csum[-1]
    padded = (counts + tm - 1) // tm * tm
    pad_end = jnp.cumsum(padded)
    dest = ((pad_end - padded)[e_flat] + rank).astype(I32)
    n_tiles = -(-(n * TOP_K + ne * (tm - 1)) // tm)
    n_rows = n_tiles * tm
    tile_expert = jnp.minimum(
        jnp.searchsorted(pad_end, jnp.arange(n_tiles, dtype=I32) * tm, side='right'), ne - 1).astype(I32)
    n_used = (pad_end[-1] // tm).astype(I32).reshape(1)
    row_tok = jnp.zeros((n_rows,), I32).at[dest].set(jnp.arange(n * TOP_K, dtype=I32) // TOP_K)
    row_gate = jnp.zeros((n_rows,), F32).at[dest].set(gates[:, :TOP_K].reshape(-1))

    s = d // LANES
    ys = _ffn(h.reshape(n * s, LANES), wg, wu, wd, tile_expert, n_used, row_gate=row_gate, row_tok=row_tok,
              add_res=False, tm=tm, tf=tf, n_rows=n_rows)
    y = _combine(x.reshape(n, s, LANES), ys.reshape(n_rows, s, LANES), dest, gamma_final, tc=combine_tile)
    return y.reshape(n, d)


def kernel(x_prompt, x_sample, cache_win0_kv, cache_win1_kv, cache_win2_kv, cache_moba_kv, page_table, rel_bias, norm_mix, norm_ffn, norm_final, w_qkv_dil, w_o_dil, w_qkv_moba, w_o_moba, w_gate_dense, w_up_dense, w_down_dense, w_router, w_gate_moe, w_up_moe, w_down_moe):
    bp, sp, d = x_prompt.shape
    bd = x_sample.shape[0]
    n_p = bp * sp
    n = n_p + bd
    assert x_sample.shape[1] == 1 and n % ROW_TILE == 0
    table = rel_bias.astype(F32)
    x0 = jnp.concatenate([x_prompt.reshape(n_p, d), x_sample.reshape(bd, d)], axis=0)

    ng = len(DIL_GROUPS)
    qkv_a = _matmul(x0, w_qkv_dil[0].astype(BF16), gamma=norm_mix[0], tm=ROW_TILE, tn=1024)
    o_p = _dil_prompt(qkv_a, table, batch=bp, seq=sp, heads=A_SLOTS, head_dim=A_HEAD_DIM)
    caches = [c[0] for c in (cache_win0_kv, cache_win1_kv, cache_win2_kv)]
    o_s = _dil_sample(qkv_a[n_p:], caches, table, heads=A_SLOTS, head_dim=A_HEAD_DIM)
    o_a = jnp.concatenate([o_p, o_s.astype(BF16)], axis=0)
    x1 = _matmul(o_a, w_o_dil[0].astype(BF16), residual=x0, tm=ROW_TILE, tn=1024)
    dense_tiles = n // ROW_TILE
    x2 = _ffn(x1, w_gate_dense.astype(BF16), w_up_dense.astype(BF16), w_down_dense.astype(BF16),
              jnp.zeros((dense_tiles,), I32), jnp.full((1,), dense_tiles, I32),
              gamma=norm_ffn[0], add_res=True, tm=ROW_TILE, tf=FF_TILE)

    qd, kd = B_HEADS * B_HEAD_DIM, B_KV_HEADS * B_HEAD_DIM
    qkv_b = _matmul(x2, w_qkv_moba[0].astype(BF16), gamma=norm_mix[1], tm=ROW_TILE, tn=1024)
    m_p = _moba_prompt(qkv_b, table, batch=bp, seq=sp, heads=B_HEADS, kv_heads=B_KV_HEADS)
    qkv_bs = qkv_b[n_p:]
    grp = B_HEADS // B_KV_HEADS
    q_s = qkv_bs[:, :qd].reshape(bd, B_HEADS, B_HEAD_DIM)
    k_s = qkv_bs[:, qd:qd + kd].reshape(bd, B_KV_HEADS, B_HEAD_DIM)
    v_s = qkv_bs[:, qd + kd:].reshape(bd, B_KV_HEADS, B_HEAD_DIM)
    m_s = _moba_sample(q_s, jnp.repeat(k_s, grp, axis=1), jnp.repeat(v_s, grp, axis=1), cache_moba_kv[0],
                       page_table, table, heads=B_HEADS, kv_heads=B_KV_HEADS)
    o_b = jnp.concatenate([m_p, m_s.reshape(bd, qd).astype(BF16)], axis=0)
    x3 = _matmul(o_b, w_o_moba[0].astype(BF16), residual=x2, tm=ROW_TILE, tn=1024)
    y = _moe(x3, norm_ffn[1], w_router[0], w_gate_moe[0].astype(BF16), w_up_moe[0].astype(BF16),
             w_down_moe[0].astype(BF16), norm_final,
             tm=MOE_ROW_TILE, tf=FF_TILE, router_tile=ROW_TILE, combine_tile=COMBINE_TILE)

    y_prompt = y[:n_p].reshape(bp, sp, d)
    y_sample = y[n_p:].reshape(bd, 1, d)
    hd = A_SLOTS * A_HEAD_DIM
    qa_s = qkv_a[n_p:].reshape(bd, 1, 3, ng, A_SLOTS, A_HEAD_DIM)
    win_p, win_s = [], []
    for g, (window, _) in enumerate(DIL_GROUPS):
        keep = min(window, sp)
        k_p, v_p = [qkv_a[:n_p, (which * ng + g) * hd:(which * ng + g + 1) * hd]
                    .reshape(bp, sp, A_SLOTS, A_HEAD_DIM)[:, sp - keep:] for which in (1, 2)]
        win_p.append(jnp.stack([k_p, v_p], axis=2)[None])
        win_s.append(jnp.stack([qa_s[:, :, 1, g], qa_s[:, :, 2, g]], axis=2)[None])
    kv_p = qkv_b[:n_p, qd:].reshape(bp, sp // PAGE_SIZE, PAGE_SIZE, 2, B_KV_HEADS, B_HEAD_DIM)[None]
    kv_s = qkv_bs[:, qd:].reshape(bd, 1, 2, B_KV_HEADS, B_HEAD_DIM)[None]
    return (y_prompt, y_sample, win_p[0], win_p[1], win_p[2], win_s[0], win_s[1], win_s[2], kv_p, kv_s)
```

```python
import functools
import math

import numpy as np
import jax
import jax.numpy as jnp
from jax import lax
from jax.experimental import pallas as pl
from jax.experimental.pallas import tpu as pltpu

F32 = jnp.float32
BF16 = jnp.bfloat16
I32 = jnp.int32

RMS_EPS = 1e-6
NEG_INF = -1e30
MIB = 1024 * 1024

N_BUCKETS = 32
MAX_DISTANCE = 2048
DIL_GROUPS = ((128, 1), (512, 4), (2048, 16))
A_SLOTS = 16
A_HEAD_DIM = 64
B_HEADS = 16
B_KV_HEADS = 4
B_HEAD_DIM = 128
MOBA_BLOCK = 256
MOBA_TOPK = 3
PAGE_SIZE = 128
N_EXPERTS = 8
TOP_K = 2

LANES = 128

ROW_TILE = 688
MOE_ROW_TILE = 768
FF_TILE = 512
W_SPLIT = 1
DIL_SAMPLE_STEP_BYTES = 8 * MIB
DIL_SAMPLE_ROWS = 16


def _params(sem, vmem_mib):
    return pltpu.CompilerParams(dimension_semantics=sem, vmem_limit_bytes=vmem_mib * MIB)


def _bucket_np(dist):
    max_exact = N_BUCKETS // 2
    n = np.maximum(np.asarray(dist, np.int64), 0)
    nf = np.maximum(n, 1).astype(np.float32)
    scale = np.float32((N_BUCKETS - max_exact) / math.log(MAX_DISTANCE / max_exact))
    large = max_exact + (np.log(nf / np.float32(max_exact)) * scale).astype(np.int32)
    return np.where(n < max_exact, n, np.minimum(large, N_BUCKETS - 1)).astype(np.int32)


def _split3(t):
    t1 = t.astype(BF16)
    r1 = t - t1.astype(F32)
    t2 = r1.astype(BF16)
    r2 = r1 - t2.astype(F32)
    return t1, t2, r2.astype(BF16)


def _aligned(x, m):
    return x if isinstance(x, int) else pl.multiple_of(x, m)


def _dot(a, b):
    return jnp.dot(a, b, preferred_element_type=F32)


def _dot_nt(a, b):
    return lax.dot_general(a, b, (((1,), (1,)), ((), ())), preferred_element_type=F32)


def _matmul(x, w, gamma=None, residual=None, *, tm, tn):
    m, k = x.shape
    n = w.shape[1]
    prenorm = gamma is not None
    stage = prenorm or x.dtype != BF16
    has_res = residual is not None
    kc = k // W_SPLIT

    def body(*refs):
        it = iter(refs)
        x_ref = next(it)
        g_ref = next(it) if prenorm else None
        w_refs = [next(it) for _ in range(W_SPLIT)]
        r_ref = next(it) if has_res else None
        o_ref = next(it)
        hb_ref = next(it) if stage else None
        if stage:
            @pl.when(pl.program_id(1) == 0)
            def _():
                xv = x_ref[...].astype(F32)
                if prenorm:
                    xv = xv * lax.rsqrt(jnp.mean(xv * xv, axis=-1, keepdims=True) + RMS_EPS) * g_ref[...]
                hb_ref[...] = xv.astype(BF16)
        a_ref = hb_ref if stage else x_ref
        acc = _dot(a_ref[:, 0:kc], w_refs[0][...])
        for c in range(1, W_SPLIT):
            acc = acc + _dot(a_ref[:, c * kc:(c + 1) * kc], w_refs[c][...])
        if has_res:
            acc = acc + r_ref[...]
        o_ref[...] = acc

    in_specs = [pl.BlockSpec((tm, k), lambda i, j: (i, 0))]
    args = [x]
    if prenorm:
        in_specs.append(pl.BlockSpec((1, k), lambda i, j: (0, 0)))
        args.append(gamma.reshape(1, k).astype(F32))
    for c in range(W_SPLIT):
        in_specs.append(pl.BlockSpec((kc, tn), lambda i, j, c=c: (c, j)))
        args.append(w)
    if has_res:
        in_specs.append(pl.BlockSpec((tm, tn), lambda i, j: (i, j)))
        args.append(residual)
    scratch = [pltpu.VMEM((tm, k), BF16)] if stage else []
    vmem = 2 * (tm * k * x.dtype.itemsize + k * tn * 2 + tm * tn * 4 * (2 if has_res else 1)) + tm * k * 2
    return pl.pallas_call(
        body,
        grid=(m // tm, n // tn),
        in_specs=in_specs,
        out_specs=pl.BlockSpec((tm, tn), lambda i, j: (i, j)),
        out_shape=jax.ShapeDtypeStruct((m, n), F32),
        scratch_shapes=scratch,
        compiler_params=_params(("arbitrary", "arbitrary"), vmem // MIB + 8),
    )(*args)


def _ffn(x, wg, wu, wd, tile_expert, n_used, gamma=None, row_gate=None, row_tok=None, *, add_res, tm, tf,
         n_rows=None):
    gather = row_tok is not None
    d = wg.shape[1]
    s = d // LANES
    m = n_rows if gather else x.shape[0]
    ff = wg.shape[2]
    nj = ff // tf
    n_tiles = m // tm
    prenorm = gamma is not None
    use_gate = row_gate is not None
    n_pf = 3 if gather else 2
    dc, fc = d // W_SPLIT, tf // W_SPLIT
    assert not (gather and (add_res or prenorm))

    def body(*refs):
        te_ref, nu_ref = refs[0], refs[1]
        tok_ref = refs[2] if gather else None
        it = iter(refs[n_pf:])
        x_ref = next(it)
        g_ref = next(it) if prenorm else None
        wg_refs = [next(it) for _ in range(W_SPLIT)]
        wu_refs = [next(it) for _ in range(W_SPLIT)]
        wd_refs = [next(it) for _ in range(W_SPLIT)]
        rg_ref = next(it) if use_gate else None
        o_ref = next(it)
        hb_ref = next(it)
        xg_ref, sems = (next(it), next(it)) if gather else (None, None)
        i = pl.program_id(0)
        j = pl.program_id(1)
        active = i < nu_ref[0]

        def row_copy(tile, slot, r):
            tok = tok_ref[tile * tm + r]
            return pltpu.make_async_copy(
                x_ref.at[pl.ds(_aligned(tok * s, s), s), :],
                xg_ref.at[slot, pl.ds(_aligned(r * s, s), s), :],
                sems.at[slot])

        def issue(tile, slot):
            def one(r, carry):
                row_copy(tile, slot, r).start()
                return carry
            lax.fori_loop(0, tm, one, 0, unroll=8)

        def drain(tile, slot):
            def one(r, carry):
                row_copy(tile, slot, r).wait()
                return carry
            lax.fori_loop(0, tm, one, 0, unroll=8)

        @pl.when(jnp.logical_and(active, j == 0))
        def _():
            if gather:
                slot = i % 2

                @pl.when(i == 0)
                def _():
                    issue(0, 0)

                drain(i, slot)
                for c in range(s):
                    hb_ref[:, c * LANES:(c + 1) * LANES] = xg_ref[slot, pl.ds(c, tm, stride=s), :].astype(BF16)

                @pl.when(i + 1 < nu_ref[0])
                def _():
                    issue(i + 1, 1 - slot)
            else:
                xv = x_ref[...]
                if prenorm:
                    xv = xv * lax.rsqrt(jnp.mean(xv * xv, axis=-1, keepdims=True) + RMS_EPS) * g_ref[...]
                hb_ref[...] = xv.astype(BF16)

        @pl.when(j == 0)
        def _():
            o_ref[...] = jnp.zeros_like(o_ref)

        @pl.when(active)
        def _():
            gt = _dot(hb_ref[:, 0:dc], wg_refs[0][0])
            up = _dot(hb_ref[:, 0:dc], wu_refs[0][0])
            for c in range(1, W_SPLIT):
                gt = gt + _dot(hb_ref[:, c * dc:(c + 1) * dc], wg_refs[c][0])
                up = up + _dot(hb_ref[:, c * dc:(c + 1) * dc], wu_refs[c][0])
            a = (gt / (1.0 + jnp.exp(-gt)) * up).astype(BF16)
            y = _dot(a[:, 0:fc], wd_refs[0][0])
            for c in range(1, W_SPLIT):
                y = y + _dot(a[:, c * fc:(c + 1) * fc], wd_refs[c][0])
            o_ref[...] += y

        @pl.when(jnp.logical_and(active, j == nj - 1))
        def _():
            r = o_ref[...]
            if use_gate:
                r = r * rg_ref[...]
            if add_res:
                r = r + x_ref[...]
            o_ref[...] = r

    def last_used(i, pf):
        return jnp.minimum(i, pf[1][0] - 1)

    def col_of(i, j, pf):
        return jnp.where(i < pf[1][0], j, nj - 1)

    if gather:
        in_specs = [pl.BlockSpec(memory_space=pl.ANY)]
    else:
        in_specs = [pl.BlockSpec((tm, d), lambda i, j, *pf: (last_used(i, pf), 0))]
    args = [x]
    if prenorm:
        in_specs.append(pl.BlockSpec((1, d), lambda i, j, *pf: (0, 0)))
        args.append(gamma.reshape(1, d).astype(F32))
    for wmat in (wg, wu):
        for c in range(W_SPLIT):
            in_specs.append(pl.BlockSpec(
                (1, dc, tf), lambda i, j, *pf, c=c: (pf[0][last_used(i, pf)], c, col_of(i, j, pf))))
            args.append(wmat)
    for c in range(W_SPLIT):
        in_specs.append(pl.BlockSpec(
            (1, fc, d), lambda i, j, *pf, c=c: (pf[0][last_used(i, pf)], col_of(i, j, pf) * W_SPLIT + c, 0)))
        args.append(wd)
    if use_gate:
        in_specs.append(pl.BlockSpec((tm, 1), lambda i, j, *pf: (last_used(i, pf), 0)))
        args.append(row_gate.reshape(m, 1))
    scratch = [pltpu.VMEM((tm, d), BF16)]
    vmem = 2 * (3 * d * tf * 2 + tm * d * 4 + (tm * LANES * 4 if use_gate else 0)) + tm * d * 2 + 2 * tm * d * 4
    if gather:
        scratch += [pltpu.VMEM((2, tm * s, LANES), F32), pltpu.SemaphoreType.DMA((2,))]
    prefetch = (tile_expert, n_used) + ((row_tok,) if gather else ())
    return pl.pallas_call(
        body,
        grid_spec=pltpu.PrefetchScalarGridSpec(
            num_scalar_prefetch=n_pf,
            grid=(n_tiles, nj),
            in_specs=in_specs,
            out_specs=pl.BlockSpec((tm, d), lambda i, j, *pf: (i, 0)),
            scratch_shapes=scratch,
        ),
        out_shape=jax.ShapeDtypeStruct((m, d), F32),
        compiler_params=_params(("arbitrary", "arbitrary"), vmem // MIB + 10),
    )(*prefetch, *args)


def _dil_bucket_maps(w, dil):
    i = np.arange(w)[:, None]
    j = np.arange(2 * w)[None, :]
    delta = i + w - j
    valid = (delta >= 0) & (delta <= w)
    bk = np.where(valid, _bucket_np(delta * dil), -1).astype(np.int32)
    first = np.where(j >= w, bk, -1).astype(np.int32)
    return np.stack([first, bk], axis=0)


def _dil_prompt(qkv, table, *, batch, seq, heads, head_dim):
    hd = heads * head_dim
    ng = len(DIL_GROUPS)
    nhp = hd // LANES
    w = DIL_GROUPS[0][0] // DIL_GROUPS[0][1]
    assert all(win // dil == w and seq % (dil * w) == 0 for win, dil in DIL_GROUPS) and 2 * head_dim == LANES
    n_iter = seq // w
    scale = head_dim ** -0.5
    bkt = jnp.asarray(np.stack([_dil_bucket_maps(w, dil) for _, dil in DIL_GROUPS]))
    merge_rows = 256

    def body(table_ref, bkt_ref, *refs):
        q_refs, k_refs, v_refs = refs[0:ng], refs[ng:2 * ng], refs[2 * ng:3 * ng]
        o_ref, bias_ref, og_ref, lg_ref = refs[3 * ng:]
        hp = pl.program_id(0)

        @pl.when(pl.program_id(1) == 0)
        def _():
            for g in range(ng):
                for v in range(2):
                    bk = bkt_ref[g, v]
                    for half in range(2):
                        t = lax.fori_loop(
                            0, N_BUCKETS, lambda k, t: jnp.where(bk == k, table_ref[k, 2 * hp + half], t),
                            jnp.zeros(bk.shape, F32))
                        bias_ref[g, v, half] = jnp.where(bk < 0, NEG_INF, t)

        lane = lax.broadcasted_iota(I32, (w, LANES), 1)
        low = lane < head_dim
        for g, (_, dil) in enumerate(DIL_GROUPS):
            nb = seq // (dil * w)

            def one_block(idx, carry, g=g, dil=dil, nb=nb):
                r = idx // nb
                n = idx - r * nb
                if dil == 1:
                    rows_c = pl.ds(_aligned(n * w, w), w)
                    rows_p = pl.ds(_aligned(jnp.maximum(n - 1, 0) * w, w), w)
                else:
                    rows_c = pl.ds(n * (w * dil) + r, w, stride=dil)
                    rows_p = pl.ds(jnp.maximum(n - 1, 0) * (w * dil) + r, w, stride=dil)
                q2 = q_refs[g][rows_c, :] * scale
                k2 = jnp.concatenate([k_refs[g][rows_p, :], k_refs[g][rows_c, :]], axis=0).astype(BF16)
                v2 = jnp.concatenate([v_refs[g][rows_p, :], v_refs[g][rows_c, :]], axis=0).astype(BF16)
                variant = jnp.minimum(n, 1)
                outs, lses = [], []
                for half in range(2):
                    keep = low if half == 0 else jnp.logical_not(low)
                    qm = jnp.where(keep, q2, 0.0).astype(BF16)
                    sc = _dot_nt(qm, k2) + bias_ref[g, variant, half]
                    mx = jnp.max(sc, axis=-1, keepdims=True)
                    e = jnp.exp(sc - mx)
                    den = jnp.sum(e, axis=-1, keepdims=True)
                    outs.append(_dot(e.astype(BF16), v2) / den)
                    lses.append(mx + jnp.log(den))
                og_ref[g, rows_c, :] = jnp.where(low, outs[0], outs[1])
                lg_ref[g, rows_c, :] = jnp.where(low, lses[0], lses[1])
                return carry

            lax.fori_loop(0, n_iter, one_block, 0, unroll=4)

        def merge(c, carry):
            rows = pl.ds(_aligned(c * merge_rows, merge_rows), merge_rows)
            ls = [lg_ref[g, rows, :] for g in range(ng)]
            mx = functools.reduce(jnp.maximum, ls)
            ws = [jnp.exp(l - mx) for l in ls]
            num = functools.reduce(lambda a, b: a + b, [ws[g] * og_ref[g, rows, :] for g in range(ng)])
            o_ref[rows, :] = (num / functools.reduce(lambda a, b: a + b, ws)).astype(o_ref.dtype)
            return carry

        lax.fori_loop(0, seq // merge_rows, merge, 0)

    def blk(which, g):
        return pl.BlockSpec((seq, LANES), lambda hp, b: (b, (which * ng + g) * nhp + hp))

    in_specs = [pl.BlockSpec(memory_space=pltpu.SMEM),
                pl.BlockSpec((ng, 2, w, 2 * w), lambda hp, b: (0, 0, 0, 0))]
    in_specs += [blk(which, g) for which in range(3) for g in range(ng)]
    return pl.pallas_call(
        body,
        grid=(nhp, batch),
        in_specs=in_specs,
        out_specs=pl.BlockSpec((seq, LANES), lambda hp, b: (b, hp)),
        out_shape=jax.ShapeDtypeStruct((batch * seq, hd), BF16),
        scratch_shapes=[pltpu.VMEM((ng, 2, 2, w, 2 * w), F32), pltpu.VMEM((ng, seq, LANES), F32),
                        pltpu.VMEM((ng, seq, LANES), F32)],
        compiler_params=_params(("arbitrary", "arbitrary"), 48),
    )(table, bkt, *([qkv] * (3 * ng)))


def _dil_sample_group(qkv3, cache, table, g, *, heads, head_dim, head_chunk, batch_chunk):
    bd = qkv3.shape[0]
    ng = len(DIL_GROUPS)
    hd = heads * head_dim
    window, dil = DIL_GROUPS[g]
    rows = head_chunk * head_dim
    mr = DIL_SAMPLE_ROWS
    parts = 4
    assert cache.shape[1] == window and heads % head_chunk == 0 and bd % batch_chunk == 0
    assert head_chunk in (8, 16) and rows % (parts * 8) == 0 and rows % LANES == 0
    kv_t = jnp.transpose(cache, (0, 2, 3, 4, 1)).reshape(bd, 2, hd, window)
    w_idx = np.arange(window)
    bias = jnp.where(jnp.asarray(w_idx % dil == 0)[None, :], table[_bucket_np(window - w_idx)].T, NEG_INF)
    bias0 = table[0][:, None]
    n_chunks = heads // head_chunk
    scale = head_dim ** -0.5

    def body(bias_ref, b0_ref, q_ref, kn_ref, vn_ref, *refs):
        k_refs, v_refs = refs[0:parts], refs[parts:2 * parts]
        o_ref, l_ref = refs[2 * parts:]
        h0 = _aligned(pl.program_id(1) * head_chunk, head_chunk)
        own = (lax.broadcasted_iota(I32, (mr, rows), 1) // head_dim) == lax.broadcasted_iota(I32, (mr, rows), 0)
        own_c = own[0:head_chunk]
        bias_c = bias_ref[pl.ds(h0, head_chunk), :]
        bias0_c = b0_ref[pl.ds(h0, head_chunk), :]

        def one_sequence(ib):
            q = q_ref[ib] * scale
            kt = jnp.concatenate([r[ib] for r in k_refs], axis=0).astype(BF16)
            vt = jnp.concatenate([r[ib] for r in v_refs], axis=0).astype(BF16)
            q_bd = jnp.where(own, jnp.broadcast_to(q, (mr, rows)), 0.0).astype(BF16)
            sc = _dot(q_bd, kt)[0:head_chunk] + bias_c
            s_new = jnp.sum(jnp.where(own_c, q * kn_ref[ib], 0.0), axis=1, keepdims=True) + bias0_c
            mx = jnp.maximum(jnp.max(sc, axis=1, keepdims=True), s_new)
            e = jnp.exp(sc - mx)
            e_new = jnp.exp(s_new - mx)
            den = jnp.sum(e, axis=1, keepdims=True) + e_new
            if head_chunk < mr:
                e = jnp.concatenate([e, jnp.zeros((mr - head_chunk, window), F32)], axis=0)
            pv = _dot_nt(e.astype(BF16), vt)[0:head_chunk]
            o_rows = jnp.where(own_c, (pv + e_new * vn_ref[ib]) / den, 0.0)
            o_ref[ib] = jnp.sum(o_rows, axis=0, keepdims=True)
            l_ref[ib] = jnp.sum(jnp.where(own_c, mx + jnp.log(den), 0.0), axis=0, keepdims=True)

        if batch_chunk == 1:
            one_sequence(0)
        else:
            def step(ib, carry):
                one_sequence(ib)
                return carry
            lax.fori_loop(0, batch_chunk, step, 0)

    def part_spec(which, p):
        return pl.BlockSpec((batch_chunk, None, rows // parts, window),
                            lambda b, hc: (b, which, hc * parts + p, 0))

    def row_spec(which):
        return pl.BlockSpec((batch_chunk, 1, rows), lambda b, hc: (b, 0, (which * ng + g) * n_chunks + hc))

    out_spec = pl.BlockSpec((batch_chunk, 1, rows), lambda b, hc: (b, 0, hc))
    out_sds = jax.ShapeDtypeStruct((bd, 1, hd), F32)
    return pl.pallas_call(
        body,
        grid=(bd // batch_chunk, n_chunks),
        in_specs=[pl.BlockSpec((heads, window), lambda b, hc: (0, 0)),
                  pl.BlockSpec((heads, 1), lambda b, hc: (0, 0)),
                  row_spec(0), row_spec(1), row_spec(2)]
        + [part_spec(0, p) for p in range(parts)] + [part_spec(1, p) for p in range(parts)],
        out_specs=[out_spec, out_spec],
        out_shape=[out_sds, out_sds],
        compiler_params=_params(("arbitrary", "arbitrary"), 40),
    )(bias, bias0, qkv3, qkv3, qkv3, *([kv_t] * (2 * parts)))


def _dil_sample(qkv_s, caches, table, *, heads, head_dim):
    bd = qkv_s.shape[0]
    ng = len(DIL_GROUPS)
    hd = heads * head_dim
    qkv3 = qkv_s.reshape(bd, 1, 3 * ng * hd)
    outs, lses = [], []
    for g, c in enumerate(caches):
        head_bytes = 2 * head_dim * DIL_GROUPS[g][0] * 4
        chunk = 16 if 16 * head_bytes <= DIL_SAMPLE_STEP_BYTES else 8
        seqs = max(1, DIL_SAMPLE_STEP_BYTES // (head_bytes * chunk))
        while bd % seqs:
            seqs -= 1
        o, lse = _dil_sample_group(qkv3, c, table, g, heads=heads, head_dim=head_dim, head_chunk=chunk,
                                   batch_chunk=seqs)
        outs.append(o.reshape(bd, hd))
        lses.append(lse.reshape(bd, hd))

    def body(*refs):
        o_refs, l_refs, out_ref = refs[:ng], refs[ng:2 * ng], refs[2 * ng]
        ls = [r[...] for r in l_refs]
        mx = functools.reduce(jnp.maximum, ls)
        ws = [jnp.exp(l - mx) for l in ls]
        num = functools.reduce(lambda a, b: a + b, [wt * r[...] for wt, r in zip(ws, o_refs)])
        out_ref[...] = num / functools.reduce(lambda a, b: a + b, ws)

    spec = pl.BlockSpec((bd, hd), lambda i: (0, 0))
    return pl.pallas_call(
        body,
        grid=(1,),
        in_specs=[spec] * (2 * ng),
        out_specs=spec,
        out_shape=jax.ShapeDtypeStruct((bd, hd), F32),
        compiler_params=_params(("arbitrary",), 32),
    )(*outs, *lses)


def _block_rank_select(gates, n_sel):
    sels = []
    nblk = len(gates)
    for k in range(nblk):
        rank = jnp.zeros(gates[k].shape, F32)
        for m in range(nblk):
            if m == k:
                continue
            beats = (gates[m] >= gates[k]) if m < k else (gates[m] > gates[k])
            rank = rank + beats.astype(F32)
        sels.append(rank < n_sel)
    return sels


def _moba_prompt(qkv, table, *, batch, seq, heads, kv_heads):
    d = B_HEAD_DIM
    blk = MOBA_BLOCK
    nblk = seq // blk
    grp = heads // kv_heads
    n_sel = min(MOBA_TOPK, nblk)
    scale = d ** -0.5
    r = np.arange(blk)[:, None]
    c = np.arange(blk)[None, :]
    bkt = jnp.asarray(np.stack([_bucket_np(db * blk + r - c) for db in range(nblk)]))

    def body(table_ref, bkt_ref, q_ref, k_ref, v_ref, o_ref, bias_ref, ka_ref, va_ref):
        h = pl.program_id(0)

        @pl.when(pl.program_id(1) == 0)
        def _():
            row = lax.broadcasted_iota(I32, (blk, blk), 0)
            col = lax.broadcasted_iota(I32, (blk, blk), 1)
            for db in range(nblk):
                bk = bkt_ref[db]
                t = lax.fori_loop(0, N_BUCKETS, lambda k, t: jnp.where(bk == k, table_ref[k, h], t),
                                  jnp.zeros((blk, blk), F32))
                if db == 0:
                    t = jnp.where(col > row, NEG_INF, t)
                bias_ref[db] = t

        lane = lax.broadcasted_iota(I32, (blk, d), 1)
        row16 = lax.broadcasted_iota(I32, (16, d), 0)
        kmean = jnp.zeros((16, d), F32)
        for n in range(nblk):
            rows = slice(n * blk, (n + 1) * blk)
            kn = k_ref[rows, :]
            ka_ref[rows, 0:d] = kn.astype(BF16)
            ka_ref[rows, d:2 * d] = jnp.where(lane == n, 1.0, 0.0).astype(BF16)
            va_ref[rows, 0:d] = v_ref[rows, :].astype(BF16)
            va_ref[rows, d:2 * d] = jnp.where(lane == 0, 1.0, 0.0).astype(BF16)
            kmean = jnp.where(row16 == n, jnp.mean(kn, axis=0, keepdims=True), kmean)
        k1, k2, k3 = _split3(kmean)
        row128 = lax.broadcasted_iota(I32, (LANES, blk), 0)

        for i in range(nblk):
            q = q_ref[i * blk:(i + 1) * blk, :]
            qa = (q * scale).astype(BF16)
            ranked = i > n_sel
            if ranked:
                q1, q2, q3 = _split3(q)
                gate_t = (_dot_nt(k1, q1) + _dot_nt(k1, q2) + _dot_nt(k2, q1)
                          + _dot_nt(k2, q2) + _dot_nt(k1, q3) + _dot_nt(k3, q1))
                sels = _block_rank_select([gate_t[n:n + 1, :] for n in range(i)], n_sel)
                sel_t = jnp.where(row128 == i, 1.0, 0.0)
                for n in range(i):
                    sel_t = jnp.where((row128 == n) & sels[n], 1.0, sel_t)
                sel_neg = jnp.where(sel_t.T > 0.5, 0.0, NEG_INF).astype(BF16)
                qa = jnp.concatenate([qa, sel_neg], axis=1)
            acc = None
            mx = None
            for n in range(i + 1):
                rows = slice(n * blk, (n + 1) * blk)
                kk = ka_ref[rows, :] if ranked else ka_ref[rows, 0:d]
                sc = _dot_nt(qa, kk) + bias_ref[i - n]
                blk_max = jnp.max(sc, axis=-1, keepdims=True)
                if n == 0:
                    mx = blk_max
                    acc = _dot(jnp.exp(sc - mx).astype(BF16), va_ref[rows, :])
                else:
                    m_new = jnp.maximum(mx, blk_max)
                    acc = jnp.exp(mx - m_new) * acc + _dot(jnp.exp(sc - m_new).astype(BF16), va_ref[rows, :])
                    mx = m_new
            o_ref[i * blk:(i + 1) * blk, :] = (acc[:, 0:d] / acc[:, d:d + 1]).astype(o_ref.dtype)

    return pl.pallas_call(
        body,
        grid=(heads, batch),
        in_specs=[
            pl.BlockSpec(memory_space=pltpu.SMEM),
            pl.BlockSpec((nblk, blk, blk), lambda h, b: (0, 0, 0)),
            pl.BlockSpec((seq, d), lambda h, b: (b, h)),
            pl.BlockSpec((seq, d), lambda h, b: (b, heads + h // grp)),
            pl.BlockSpec((seq, d), lambda h, b: (b, heads + kv_heads + h // grp)),
        ],
        out_specs=pl.BlockSpec((seq, d), lambda h, b: (b, h)),
        out_shape=jax.ShapeDtypeStruct((batch * seq, heads * d), BF16),
        scratch_shapes=[pltpu.VMEM((nblk, blk, blk), F32), pltpu.VMEM((seq, 2 * d), BF16),
                        pltpu.VMEM((seq, 2 * d), BF16)],
        compiler_params=_params(("arbitrary", "arbitrary"), 40),
    )(table, bkt, qkv, qkv, qkv)


def _moba_sample(q, k_new, v_new, cache, page_table, table, *, heads, kv_heads):
    bd = q.shape[0]
    d = B_HEAD_DIM
    blk = MOBA_BLOCK
    n_pages = page_table.shape[1]
    past = n_pages * PAGE_SIZE
    nblk = past // blk
    rpt = 2 * kv_heads
    assert blk == 2 * PAGE_SIZE and past % blk == 0 and rpt == 8
    grp = heads // kv_heads
    n_sel = min(MOBA_TOPK, nblk + 1)
    scale = d ** -0.5
    page_rows = PAGE_SIZE * rpt
    cols = blk * rpt
    pages = cache.reshape(cache.shape[0], page_rows, d)
    oh = np.zeros((LANES, past), np.float32)
    oh[_bucket_np(past - np.arange(past)), np.arange(past)] = 1.0
    onehot_t = jnp.asarray(oh, BF16)
    ex = np.zeros((blk, cols), np.float32)
    ex[np.arange(cols) // rpt, np.arange(cols)] = 1.0
    expand = jnp.asarray(ex, BF16)
    t_t = jnp.zeros((heads, LANES), F32).at[:, :N_BUCKETS].set(table.astype(F32).T)

    sub = 4 if nblk % 4 == 0 else 1
    psplit = 2
    part_rows = page_rows // psplit
    n_parts = sub * 2 * psplit

    def body(pt_ref, tt_ref, oh_ref, ex_ref, q_ref, kn_ref, vn_ref, *refs):
        part_refs = refs[:n_parts]
        o_ref, bias_ref, m_ref, l_ref, acc_ref, gate_ref = refs[n_parts:]
        b = pl.program_id(0)
        step = pl.program_id(1)

        @pl.when((b == 0) & (step == 0))
        def _():
            t1, t2, t3 = _split3(tt_ref[...])
            full = _dot(t1, oh_ref[...]) + _dot(t2, oh_ref[...]) + _dot(t3, oh_ref[...])
            col = lax.broadcasted_iota(I32, (heads, cols), 1)
            row = lax.broadcasted_iota(I32, (heads, cols), 0)
            own_k_row = (col % rpt) == (row // grp)
            for k in range(nblk):
                f1, f2, f3 = _split3(full[:, k * blk:(k + 1) * blk])
                bias_ref[k] = jnp.where(own_k_row, _dot(f1, ex_ref[...]) + _dot(f2, ex_ref[...])
                                        + _dot(f3, ex_ref[...]), NEG_INF)

        qv = q_ref[0]
        qb = (qv * scale).astype(BF16)
        row_grp = lax.broadcasted_iota(I32, (heads, 1), 0) // grp
        for kb in range(sub):
            n = step * sub + kb
            parts = part_refs[kb * 2 * psplit:(kb + 1) * 2 * psplit]
            x = jnp.concatenate([p[0] for p in parts], axis=0)
            xb = x.astype(BF16)
            sc = _dot_nt(qb, xb) + bias_ref[n]
            mx = jnp.max(sc, axis=1, keepdims=True)
            e = jnp.exp(sc - mx)
            den = jnp.sum(e, axis=1, keepdims=True)
            acc = _dot(pltpu.roll(e, kv_heads, axis=1).astype(BF16), xb)
            ksum = jnp.sum(x.reshape(blk, rpt, d), axis=0)
            gate = jnp.zeros((heads, 1), F32)
            for g in range(kv_heads):
                gg = jnp.sum(qv * (ksum[g:g + 1, :] * (1.0 / blk)), axis=1, keepdims=True)
                gate = jnp.where(row_grp == g, gg, gate)
            m_ref[n] = jnp.broadcast_to(mx, (heads, d))
            l_ref[n] = jnp.broadcast_to(den, (heads, d))
            gate_ref[n] = jnp.broadcast_to(gate, (heads, d))
            acc_ref[n] = acc

        @pl.when(step == nblk // sub - 1)
        def _():
            gates = [gate_ref[k] for k in range(nblk)]
            sels = _block_rank_select(gates, n_sel)
            s_own = jnp.sum(qv * kn_ref[0], axis=1, keepdims=True) * scale + tt_ref[:, 0:1]
            big = jnp.broadcast_to(s_own, (heads, d))
            for k in range(nblk):
                big = jnp.maximum(big, jnp.where(sels[k], m_ref[k], NEG_INF))
            w_own = jnp.exp(s_own - big)
            tot = w_own
            out = w_own * vn_ref[0]
            for k in range(nblk):
                wk = jnp.where(sels[k], jnp.exp(m_ref[k] - big), 0.0)
                tot = tot + wk * l_ref[k]
                out = out + wk * acc_ref[k]
            o_ref[0] = out / tot

    def part_spec(which):
        page, part = which // psplit, which % psplit
        return pl.BlockSpec((1, part_rows, d),
                            lambda b, n, pt: (pt[b * n_pages + 2 * sub * n + page], part, 0))

    row3 = pl.BlockSpec((1, heads, d), lambda b, n, pt: (b, 0, 0))
    stat = pltpu.VMEM((nblk, heads, d), F32)
    return pl.pallas_call(
        body,
        grid_spec=pltpu.PrefetchScalarGridSpec(
            num_scalar_prefetch=1,
            grid=(bd, nblk // sub),
            in_specs=[
                pl.BlockSpec((heads, LANES), lambda b, n, pt: (0, 0)),
                pl.BlockSpec((LANES, past), lambda b, n, pt: (0, 0)),
                pl.BlockSpec((blk, cols), lambda b, n, pt: (0, 0)),
                row3, row3, row3] + [part_spec(which) for which in range(n_parts)],
            out_specs=row3,
            scratch_shapes=[pltpu.VMEM((nblk, heads, cols), F32), stat, stat, stat, stat],
        ),
        out_shape=jax.ShapeDtypeStruct((bd, heads, d), F32),
        compiler_params=_params(("arbitrary", "arbitrary"), 32),
    )(page_table.reshape(-1).astype(I32), t_t, onehot_t, expand, q, k_new, v_new, *([pages] * n_parts))


def _router(x, gamma, w_router, *, tm):
    m, d = x.shape
    ne = w_router.shape[1]
    w_pad = jnp.zeros((d, LANES), F32).at[:, :ne].set(w_router.astype(F32))

    def body(x_ref, g_ref, w_ref, h_ref, idx_ref, gate_ref):
        xv = x_ref[...]
        h = xv * lax.rsqrt(jnp.mean(xv * xv, axis=-1, keepdims=True) + RMS_EPS) * g_ref[...]
        h_ref[...] = h
        h1, h2, h3 = _split3(h)
        w1, w2, w3 = _split3(w_ref[...])
        lg = (_dot(h1, w1) + _dot(h1, w2) + _dot(h2, w1) + _dot(h2, w2) + _dot(h1, w3) + _dot(h3, w1))
        lane = lax.broadcasted_iota(I32, lg.shape, 1)
        lane_f = lane.astype(F32)
        lg = jnp.where(lane < ne, lg, -jnp.inf)
        v1 = jnp.max(lg, axis=-1, keepdims=True)
        i1 = jnp.min(jnp.where(lg == v1, lane_f, float(LANES)), axis=-1, keepdims=True)
        lg2 = jnp.where(lane_f == i1, -jnp.inf, lg)
        v2 = jnp.max(lg2, axis=-1, keepdims=True)
        i2 = jnp.min(jnp.where(lg2 == v2, lane_f, float(LANES)), axis=-1, keepdims=True)
        e2 = jnp.exp(v2 - v1)
        g1 = 1.0 / (1.0 + e2)
        g2 = e2 / (1.0 + e2)
        idx_ref[...] = jnp.where(lane == 0, i1, jnp.where(lane == 1, i2, 0.0)).astype(I32)
        gate_ref[...] = jnp.where(lane == 0, g1, jnp.where(lane == 1, g2, 0.0))

    return pl.pallas_call(
        body,
        grid=(m // tm,),
        in_specs=[pl.BlockSpec((tm, d), lambda i: (i, 0)), pl.BlockSpec((1, d), lambda i: (0, 0)),
                  pl.BlockSpec((d, LANES), lambda i: (0, 0))],
        out_specs=[pl.BlockSpec((tm, d), lambda i: (i, 0)), pl.BlockSpec((tm, LANES), lambda i: (i, 0)),
                   pl.BlockSpec((tm, LANES), lambda i: (i, 0))],
        out_shape=[jax.ShapeDtypeStruct((m, d), F32), jax.ShapeDtypeStruct((m, LANES), I32),
                   jax.ShapeDtypeStruct((m, LANES), F32)],
        compiler_params=_params(("arbitrary",), 40),
    )(x, gamma.reshape(1, d).astype(F32), w_pad)


def _combine(x, ys_slab, dest, gamma, *, tc, n_head):
    n, d = x.shape
    s = d // LANES
    n_tiles = n // tc
    head_tiles = n_head // tc
    assert n % tc == 0 and n_head % tc == 0 and n - n_head == tc

    def body(dest_ref, x_ref, ys_ref, g_ref, oh_ref, ot_ref, buf, acc_ref, sems):
        i = pl.program_id(0)
        slot = i % 2

        def copy(tile, sl, r, k):
            row = dest_ref[2 * (tile * tc + r) + k]
            return pltpu.make_async_copy(ys_ref.at[pl.ds(_aligned(row * s, s), s), :],
                                         buf.at[sl, k, pl.ds(_aligned(r * s, s), s), :], sems.at[sl])

        def issue(tile, sl):
            def one(r, carry):
                copy(tile, sl, r, 0).start()
                copy(tile, sl, r, 1).start()
                return carry
            lax.fori_loop(0, tc, one, 0, unroll=8)

        def drain(tile, sl):
            def one(r, carry):
                copy(tile, sl, r, 0).wait()
                copy(tile, sl, r, 1).wait()
                return carry
            lax.fori_loop(0, tc, one, 0, unroll=8)

        @pl.when(i == 0)
        def _():
            issue(0, 0)

        drain(i, slot)

        @pl.when(i + 1 < n_tiles)
        def _():
            issue(i + 1, 1 - slot)

        sumsq = jnp.zeros((tc, 1), F32)
        for c in range(s):
            cols = slice(c * LANES, (c + 1) * LANES)
            xv = (x_ref[:, cols] + buf[slot, 0, pl.ds(c, tc, stride=s), :]
                  + buf[slot, 1, pl.ds(c, tc, stride=s), :])
            acc_ref[:, cols] = xv
            sumsq = sumsq + jnp.sum(xv * xv, axis=1, keepdims=True)
        y = acc_ref[...] * lax.rsqrt(sumsq * (1.0 / d) + RMS_EPS) * g_ref[...]

        @pl.when(i < head_tiles)
        def _():
            oh_ref[...] = y

        @pl.when(i >= head_tiles)
        def _():
            ot_ref[...] = y

    return pl.pallas_call(
        body,
        grid_spec=pltpu.PrefetchScalarGridSpec(
            num_scalar_prefetch=1,
            grid=(n_tiles,),
            in_specs=[pl.BlockSpec((tc, d), lambda i, dr: (i, 0)),
                      pl.BlockSpec(memory_space=pl.ANY),
                      pl.BlockSpec((1, d), lambda i, dr: (0, 0))],
            out_specs=[pl.BlockSpec((tc, d), lambda i, dr: (jnp.minimum(i, head_tiles - 1), 0)),
                       pl.BlockSpec((tc, d), lambda i, dr: (0, 0))],
            scratch_shapes=[pltpu.VMEM((2, 2, tc * s, LANES), F32), pltpu.VMEM((tc, d), F32),
                            pltpu.SemaphoreType.DMA((2,))],
        ),
        out_shape=[jax.ShapeDtypeStruct((n_head, d), F32), jax.ShapeDtypeStruct((tc, d), F32)],
        compiler_params=_params(("arbitrary",), 11 * tc * d * 4 // MIB + 8),
    )(dest, x, ys_slab, gamma.reshape(1, d).astype(F32))


def _moe(x, gamma_ffn, w_router, wg, wu, wd, gamma_final, *, tm, tf, router_tile, combine_tile, n_head):
    n, d = x.shape
    ne = wg.shape[0]
    h, idx, gates = _router(x, gamma_ffn, w_router, tm=router_tile)
    e_flat = idx[:, :TOP_K].reshape(-1)
    onehot = (e_flat[:, None] == jnp.arange(ne, dtype=I32)[None, :]).astype(I32)
    csum = jnp.cumsum(onehot, axis=0)
    rank = jnp.sum((csum - onehot) * onehot, axis=1)
    counts = csum[-1]
    padded = (counts + tm - 1) // tm * tm
    pad_end = jnp.cumsum(padded)
    dest = ((pad_end - padded)[e_flat] + rank).astype(I32)
    n_tiles = -(-(n * TOP_K + ne * (tm - 1)) // tm)
    n_rows = n_tiles * tm
    tile_expert = jnp.minimum(
        jnp.searchsorted(pad_end, jnp.arange(n_tiles, dtype=I32) * tm, side='right'), ne - 1).astype(I32)
    n_used = (pad_end[-1] // tm).astype(I32).reshape(1)
    row_tok = jnp.zeros((n_rows,), I32).at[dest].set(jnp.arange(n * TOP_K, dtype=I32) // TOP_K)
    row_gate = jnp.zeros((n_rows,), F32).at[dest].set(gates[:, :TOP_K].reshape(-1))

    s = d // LANES
    ys = _ffn(h.reshape(n * s, LANES), wg, wu, wd, tile_expert, n_used, row_gate=row_gate, row_tok=row_tok,
              add_res=False, tm=tm, tf=tf, n_rows=n_rows)
    return _combine(x, ys.reshape(n_rows * s, LANES), dest, gamma_final, tc=combine_tile, n_head=n_head)


def kernel(x_prompt, x_sample, cache_win0_kv, cache_win1_kv, cache_win2_kv, cache_moba_kv, page_table, rel_bias, norm_mix, norm_ffn, norm_final, w_qkv_dil, w_o_dil, w_qkv_moba, w_o_moba, w_gate_dense, w_up_dense, w_down_dense, w_router, w_gate_moe, w_up_moe, w_down_moe):
    bp, sp, d = x_prompt.shape
    bd = x_sample.shape[0]
    n_p = bp * sp
    n = n_p + bd
    assert x_sample.shape[1] == 1 and n % ROW_TILE == 0
    table = rel_bias.astype(F32)
    x0 = jnp.concatenate([x_prompt.reshape(n_p, d), x_sample.reshape(bd, d)], axis=0)

    ng = len(DIL_GROUPS)
    qkv_a = _matmul(x0, w_qkv_dil[0].astype(BF16), gamma=norm_mix[0], tm=ROW_TILE, tn=1024)
    o_p = _dil_prompt(qkv_a, table, batch=bp, seq=sp, heads=A_SLOTS, head_dim=A_HEAD_DIM)
    caches = [c[0] for c in (cache_win0_kv, cache_win1_kv, cache_win2_kv)]
    o_s = _dil_sample(qkv_a[n_p:], caches, table, heads=A_SLOTS, head_dim=A_HEAD_DIM)
    o_a = jnp.concatenate([o_p, o_s.astype(BF16)], axis=0)
    x1 = _matmul(o_a, w_o_dil[0].astype(BF16), residual=x0, tm=ROW_TILE, tn=1024)
    dense_tiles = n // ROW_TILE
    x2 = _ffn(x1, w_gate_dense.astype(BF16), w_up_dense.astype(BF16), w_down_dense.astype(BF16),
              jnp.zeros((dense_tiles,), I32), jnp.full((1,), dense_tiles, I32),
              gamma=norm_ffn[0], add_res=True, tm=ROW_TILE, tf=FF_TILE)

    qd, kd = B_HEADS * B_HEAD_DIM, B_KV_HEADS * B_HEAD_DIM
    qkv_b = _matmul(x2, w_qkv_moba[0].astype(BF16), gamma=norm_mix[1], tm=ROW_TILE, tn=1024)
    m_p = _moba_prompt(qkv_b, table, batch=bp, seq=sp, heads=B_HEADS, kv_heads=B_KV_HEADS)
    qkv_bs = qkv_b[n_p:]
    grp = B_HEADS // B_KV_HEADS
    q_s = qkv_bs[:, :qd].reshape(bd, B_HEADS, B_HEAD_DIM)
    k_s = qkv_bs[:, qd:qd + kd].reshape(bd, B_KV_HEADS, B_HEAD_DIM)
    v_s = qkv_bs[:, qd + kd:].reshape(bd, B_KV_HEADS, B_HEAD_DIM)
    m_s = _moba_sample(q_s, jnp.repeat(k_s, grp, axis=1), jnp.repeat(v_s, grp, axis=1), cache_moba_kv[0],
                       page_table, table, heads=B_HEADS, kv_heads=B_KV_HEADS)
    o_b = jnp.concatenate([m_p, m_s.reshape(bd, qd).astype(BF16)], axis=0)
    x3 = _matmul(o_b, w_o_moba[0].astype(BF16), residual=x2, tm=ROW_TILE, tn=1024)
    y_p, y_s = _moe(x3, norm_ffn[1], w_router[0], w_gate_moe[0].astype(BF16), w_up_moe[0].astype(BF16),
                    w_down_moe[0].astype(BF16), norm_final,
                    tm=MOE_ROW_TILE, tf=FF_TILE, router_tile=ROW_TILE, combine_tile=bd, n_head=n_p)

    y_prompt = y_p.reshape(bp, sp, d)
    y_sample = y_s.reshape(bd, 1, d)
    hd = A_SLOTS * A_HEAD_DIM
    qa_s = qkv_a[n_p:].reshape(bd, 1, 3, ng, A_SLOTS, A_HEAD_DIM)
    win_p, win_s = [], []
    for g, (window, _) in enumerate(DIL_GROUPS):
        keep = min(window, sp)
        k_p, v_p = [qkv_a[:n_p, (which * ng + g) * hd:(which * ng + g + 1) * hd]
                    .reshape(bp, sp, A_SLOTS, A_HEAD_DIM)[:, sp - keep:] for which in (1, 2)]
        win_p.append(jnp.stack([k_p, v_p], axis=2)[None])
        win_s.append(jnp.stack([qa_s[:, :, 1, g], qa_s[:, :, 2, g]], axis=2)[None])
    kv_p = qkv_b[:n_p, qd:].reshape(bp, sp // PAGE_SIZE, PAGE_SIZE, 2, B_KV_HEADS, B_HEAD_DIM)[None]
    kv_s = qkv_bs[:, qd:].reshape(bd, 1, 2, B_KV_HEADS, B_HEAD_DIM)[None]
    return (y_prompt, y_sample, win_p[0], win_p[1], win_p[2], win_s[0], win_s[1], win_s[2], kv_p, kv_s)
```

```python
import functools
import math

import numpy as np
import jax
import jax.numpy as jnp
from jax import lax
from jax.experimental import pallas as pl
from jax.experimental.pallas import tpu as pltpu

F32 = jnp.float32
BF16 = jnp.bfloat16
I32 = jnp.int32

RMS_EPS = 1e-6
NEG_INF = -1e30
MIB = 1024 * 1024

N_BUCKETS = 32
MAX_DISTANCE = 2048
DIL_GROUPS = ((128, 1), (512, 4), (2048, 16))
A_SLOTS = 16
A_HEAD_DIM = 64
B_HEADS = 16
B_KV_HEADS = 4
B_HEAD_DIM = 128
MOBA_BLOCK = 256
MOBA_TOPK = 3
PAGE_SIZE = 128
N_EXPERTS = 8
TOP_K = 2

LANES = 128

ROW_TILE = 688
MOE_ROW_TILE = 768
FF_TILE = 512
W_SPLIT = 1
DIL_SAMPLE_STEP_BYTES = 8 * MIB
DIL_SAMPLE_ROWS = 16


def _params(sem, vmem_mib):
    return pltpu.CompilerParams(dimension_semantics=sem, vmem_limit_bytes=vmem_mib * MIB)


def _bucket_np(dist):
    max_exact = N_BUCKETS // 2
    n = np.maximum(np.asarray(dist, np.int64), 0)
    nf = np.maximum(n, 1).astype(np.float32)
    scale = np.float32((N_BUCKETS - max_exact) / math.log(MAX_DISTANCE / max_exact))
    large = max_exact + (np.log(nf / np.float32(max_exact)) * scale).astype(np.int32)
    return np.where(n < max_exact, n, np.minimum(large, N_BUCKETS - 1)).astype(np.int32)


def _split3(t):
    t1 = t.astype(BF16)
    r1 = t - t1.astype(F32)
    t2 = r1.astype(BF16)
    r2 = r1 - t2.astype(F32)
    return t1, t2, r2.astype(BF16)


def _aligned(x, m):
    return x if isinstance(x, int) else pl.multiple_of(x, m)


def _dot(a, b):
    return jnp.dot(a, b, preferred_element_type=F32)


def _dot_nt(a, b):
    return lax.dot_general(a, b, (((1,), (1,)), ((), ())), preferred_element_type=F32)


def _matmul(x, w, gamma=None, residual=None, *, tm, tn):
    m, k = x.shape
    n = w.shape[1]
    prenorm = gamma is not None
    stage = prenorm or x.dtype != BF16
    has_res = residual is not None
    kc = k // W_SPLIT

    def body(*refs):
        it = iter(refs)
        x_ref = next(it)
        g_ref = next(it) if prenorm else None
        w_refs = [next(it) for _ in range(W_SPLIT)]
        r_ref = next(it) if has_res else None
        o_ref = next(it)
        hb_ref = next(it) if stage else None
        if stage:
            @pl.when(pl.program_id(1) == 0)
            def _():
                xv = x_ref[...].astype(F32)
                if prenorm:
                    xv = xv * lax.rsqrt(jnp.mean(xv * xv, axis=-1, keepdims=True) + RMS_EPS) * g_ref[...]
                hb_ref[...] = xv.astype(BF16)
        a_ref = hb_ref if stage else x_ref
        acc = _dot(a_ref[:, 0:kc], w_refs[0][...])
        for c in range(1, W_SPLIT):
            acc = acc + _dot(a_ref[:, c * kc:(c + 1) * kc], w_refs[c][...])
        if has_res:
            acc = acc + r_ref[...]
        o_ref[...] = acc

    in_specs = [pl.BlockSpec((tm, k), lambda i, j: (i, 0))]
    args = [x]
    if prenorm:
        in_specs.append(pl.BlockSpec((1, k), lambda i, j: (0, 0)))
        args.append(gamma.reshape(1, k).astype(F32))
    for c in range(W_SPLIT):
        in_specs.append(pl.BlockSpec((kc, tn), lambda i, j, c=c: (c, j)))
        args.append(w)
    if has_res:
        in_specs.append(pl.BlockSpec((tm, tn), lambda i, j: (i, j)))
        args.append(residual)
    scratch = [pltpu.VMEM((tm, k), BF16)] if stage else []
    vmem = 2 * (tm * k * x.dtype.itemsize + k * tn * 2 + tm * tn * 4 * (2 if has_res else 1)) + tm * k * 2
    return pl.pallas_call(
        body,
        grid=(m // tm, n // tn),
        in_specs=in_specs,
        out_specs=pl.BlockSpec((tm, tn), lambda i, j: (i, j)),
        out_shape=jax.ShapeDtypeStruct((m, n), F32),
        scratch_shapes=scratch,
        compiler_params=_params(("arbitrary", "arbitrary"), vmem // MIB + 8),
    )(*args)


def _ffn(x, wg, wu, wd, tile_expert, n_used, gamma=None, row_gate=None, row_tok=None, *, add_res, tm, tf,
         n_rows=None):
    gather = row_tok is not None
    d = wg.shape[1]
    s = d // LANES
    m = n_rows if gather else x.shape[0]
    ff = wg.shape[2]
    nj = ff // tf
    n_tiles = m // tm
    prenorm = gamma is not None
    use_gate = row_gate is not None
    n_pf = 3 if gather else 2
    dc, fc = d // W_SPLIT, tf // W_SPLIT
    assert not (gather and (add_res or prenorm))

    def body(*refs):
        te_ref, nu_ref = refs[0], refs[1]
        tok_ref = refs[2] if gather else None
        it = iter(refs[n_pf:])
        x_ref = next(it)
        g_ref = next(it) if prenorm else None
        wg_refs = [next(it) for _ in range(W_SPLIT)]
        wu_refs = [next(it) for _ in range(W_SPLIT)]
        wd_refs = [next(it) for _ in range(W_SPLIT)]
        rg_ref = next(it) if use_gate else None
        o_ref = next(it)
        hb_ref = next(it)
        xg_ref, sems, acc_ref = (next(it), next(it), next(it)) if gather else (None, None, o_ref)
        i = pl.program_id(0)
        j = pl.program_id(1)
        active = i < nu_ref[0]

        def row_copy(tile, slot, r):
            tok = tok_ref[tile * tm + r]
            return pltpu.make_async_copy(
                x_ref.at[pl.ds(_aligned(tok * s, s), s), :],
                xg_ref.at[slot, pl.ds(_aligned(r * s, s), s), :],
                sems.at[slot])

        def issue(tile, slot):
            def one(r, carry):
                row_copy(tile, slot, r).start()
                return carry
            lax.fori_loop(0, tm, one, 0, unroll=8)

        def drain(tile, slot):
            def one(r, carry):
                row_copy(tile, slot, r).wait()
                return carry
            lax.fori_loop(0, tm, one, 0, unroll=8)

        @pl.when(jnp.logical_and(active, j == 0))
        def _():
            if gather:
                slot = i % 2

                @pl.when(i == 0)
                def _():
                    issue(0, 0)

                drain(i, slot)
                for c in range(s):
                    hb_ref[:, c * LANES:(c + 1) * LANES] = xg_ref[slot, pl.ds(c, tm, stride=s), :].astype(BF16)

                @pl.when(i + 1 < nu_ref[0])
                def _():
                    issue(i + 1, 1 - slot)
            else:
                xv = x_ref[...]
                if prenorm:
                    xv = xv * lax.rsqrt(jnp.mean(xv * xv, axis=-1, keepdims=True) + RMS_EPS) * g_ref[...]
                hb_ref[...] = xv.astype(BF16)

        @pl.when(j == 0)
        def _():
            acc_ref[...] = jnp.zeros_like(acc_ref)

        if gather:
            @pl.when(jnp.logical_and(j == 0, jnp.logical_not(active)))
            def _():
                o_ref[...] = jnp.zeros_like(o_ref)

        @pl.when(active)
        def _():
            gt = _dot(hb_ref[:, 0:dc], wg_refs[0][0])
            up = _dot(hb_ref[:, 0:dc], wu_refs[0][0])
            for c in range(1, W_SPLIT):
                gt = gt + _dot(hb_ref[:, c * dc:(c + 1) * dc], wg_refs[c][0])
                up = up + _dot(hb_ref[:, c * dc:(c + 1) * dc], wu_refs[c][0])
            a = (gt / (1.0 + jnp.exp(-gt)) * up).astype(BF16)
            y = _dot(a[:, 0:fc], wd_refs[0][0])
            for c in range(1, W_SPLIT):
                y = y + _dot(a[:, c * fc:(c + 1) * fc], wd_refs[c][0])
            acc_ref[...] += y

        @pl.when(jnp.logical_and(active, j == nj - 1))
        def _():
            r = acc_ref[...]
            if use_gate:
                r = r * rg_ref[...]
            if add_res:
                r = r + x_ref[...]
            if gather:
                for c in range(s):
                    o_ref[pl.ds(c, tm, stride=s), :] = r[:, c * LANES:(c + 1) * LANES]
            else:
                o_ref[...] = r

    def last_used(i, pf):
        return jnp.minimum(i, pf[1][0] - 1)

    def col_of(i, j, pf):
        return jnp.where(i < pf[1][0], j, nj - 1)

    if gather:
        in_specs = [pl.BlockSpec(memory_space=pl.ANY)]
    else:
        in_specs = [pl.BlockSpec((tm, d), lambda i, j, *pf: (last_used(i, pf), 0))]
    args = [x]
    if prenorm:
        in_specs.append(pl.BlockSpec((1, d), lambda i, j, *pf: (0, 0)))
        args.append(gamma.reshape(1, d).astype(F32))
    for wmat in (wg, wu):
        for c in range(W_SPLIT):
            in_specs.append(pl.BlockSpec(
                (1, dc, tf), lambda i, j, *pf, c=c: (pf[0][last_used(i, pf)], c, col_of(i, j, pf))))
            args.append(wmat)
    for c in range(W_SPLIT):
        in_specs.append(pl.BlockSpec(
            (1, fc, d), lambda i, j, *pf, c=c: (pf[0][last_used(i, pf)], col_of(i, j, pf) * W_SPLIT + c, 0)))
        args.append(wd)
    if use_gate:
        in_specs.append(pl.BlockSpec((tm, 1), lambda i, j, *pf: (last_used(i, pf), 0)))
        args.append(row_gate.reshape(m, 1))
    scratch = [pltpu.VMEM((tm, d), BF16)]
    vmem = 2 * (3 * d * tf * 2 + tm * d * 4 + (tm * LANES * 4 if use_gate else 0)) + tm * d * 2 + 2 * tm * d * 4
    if gather:
        scratch += [pltpu.VMEM((2, tm * s, LANES), F32), pltpu.SemaphoreType.DMA((2,)), pltpu.VMEM((tm, d), F32)]
        vmem += tm * d * 4
        out_spec = pl.BlockSpec((tm * s, LANES), lambda i, j, *pf: (i, 0))
        out_sds = jax.ShapeDtypeStruct((m * s, LANES), F32)
    else:
        out_spec = pl.BlockSpec((tm, d), lambda i, j, *pf: (i, 0))
        out_sds = jax.ShapeDtypeStruct((m, d), F32)
    prefetch = (tile_expert, n_used) + ((row_tok,) if gather else ())
    return pl.pallas_call(
        body,
        grid_spec=pltpu.PrefetchScalarGridSpec(
            num_scalar_prefetch=n_pf,
            grid=(n_tiles, nj),
            in_specs=in_specs,
            out_specs=out_spec,
            scratch_shapes=scratch,
        ),
        out_shape=out_sds,
        compiler_params=_params(("arbitrary", "arbitrary"), vmem // MIB + 10),
    )(*prefetch, *args)


def _dil_bucket_maps(w, dil):
    i = np.arange(w)[:, None]
    j = np.arange(2 * w)[None, :]
    delta = i + w - j
    valid = (delta >= 0) & (delta <= w)
    bk = np.where(valid, _bucket_np(delta * dil), -1).astype(np.int32)
    first = np.where(j >= w, bk, -1).astype(np.int32)
    return np.stack([first, bk], axis=0)


def _dil_prompt(qkv, table, *, batch, seq, heads, head_dim):
    hd = heads * head_dim
    ng = len(DIL_GROUPS)
    nhp = hd // LANES
    w = DIL_GROUPS[0][0] // DIL_GROUPS[0][1]
    assert all(win // dil == w and seq % (dil * w) == 0 for win, dil in DIL_GROUPS) and 2 * head_dim == LANES
    n_iter = seq // w
    scale = head_dim ** -0.5
    bkt = jnp.asarray(np.stack([_dil_bucket_maps(w, dil) for _, dil in DIL_GROUPS]))
    merge_rows = 256

    def body(table_ref, bkt_ref, *refs):
        q_refs, k_refs, v_refs = refs[0:ng], refs[ng:2 * ng], refs[2 * ng:3 * ng]
        o_ref, bias_ref, og_ref, lg_ref = refs[3 * ng:]
        hp = pl.program_id(0)

        @pl.when(pl.program_id(1) == 0)
        def _():
            for g in range(ng):
                for v in range(2):
                    bk = bkt_ref[g, v]
                    for half in range(2):
                        t = lax.fori_loop(
                            0, N_BUCKETS, lambda k, t: jnp.where(bk == k, table_ref[k, 2 * hp + half], t),
                            jnp.zeros(bk.shape, F32))
                        bias_ref[g, v, half] = jnp.where(bk < 0, NEG_INF, t)

        lane = lax.broadcasted_iota(I32, (w, LANES), 1)
        low = lane < head_dim
        for g, (_, dil) in enumerate(DIL_GROUPS):
            nb = seq // (dil * w)

            def one_block(idx, carry, g=g, dil=dil, nb=nb):
                r = idx // nb
                n = idx - r * nb
                if dil == 1:
                    rows_c = pl.ds(_aligned(n * w, w), w)
                    rows_p = pl.ds(_aligned(jnp.maximum(n - 1, 0) * w, w), w)
                else:
                    rows_c = pl.ds(n * (w * dil) + r, w, stride=dil)
                    rows_p = pl.ds(jnp.maximum(n - 1, 0) * (w * dil) + r, w, stride=dil)
                q2 = q_refs[g][rows_c, :] * scale
                k2 = jnp.concatenate([k_refs[g][rows_p, :], k_refs[g][rows_c, :]], axis=0).astype(BF16)
                v2 = jnp.concatenate([v_refs[g][rows_p, :], v_refs[g][rows_c, :]], axis=0).astype(BF16)
                variant = jnp.minimum(n, 1)
                outs, lses = [], []
                for half in range(2):
                    keep = low if half == 0 else jnp.logical_not(low)
                    qm = jnp.where(keep, q2, 0.0).astype(BF16)
                    sc = _dot_nt(qm, k2) + bias_ref[g, variant, half]
                    mx = jnp.max(sc, axis=-1, keepdims=True)
                    e = jnp.exp(sc - mx)
                    den = jnp.sum(e, axis=-1, keepdims=True)
                    outs.append(_dot(e.astype(BF16), v2) / den)
                    lses.append(mx + jnp.log(den))
                og_ref[g, rows_c, :] = jnp.where(low, outs[0], outs[1])
                lg_ref[g, rows_c, :] = jnp.where(low, lses[0], lses[1])
                return carry

            lax.fori_loop(0, n_iter, one_block, 0, unroll=4)

        def merge(c, carry):
            rows = pl.ds(_aligned(c * merge_rows, merge_rows), merge_rows)
            ls = [lg_ref[g, rows, :] for g in range(ng)]
            mx = functools.reduce(jnp.maximum, ls)
            ws = [jnp.exp(l - mx) for l in ls]
            num = functools.reduce(lambda a, b: a + b, [ws[g] * og_ref[g, rows, :] for g in range(ng)])
            o_ref[rows, :] = (num / functools.reduce(lambda a, b: a + b, ws)).astype(o_ref.dtype)
            return carry

        lax.fori_loop(0, seq // merge_rows, merge, 0)

    def blk(which, g):
        return pl.BlockSpec((seq, LANES), lambda hp, b: (b, (which * ng + g) * nhp + hp))

    in_specs = [pl.BlockSpec(memory_space=pltpu.SMEM),
                pl.BlockSpec((ng, 2, w, 2 * w), lambda hp, b: (0, 0, 0, 0))]
    in_specs += [blk(which, g) for which in range(3) for g in range(ng)]
    return pl.pallas_call(
        body,
        grid=(nhp, batch),
        in_specs=in_specs,
        out_specs=pl.BlockSpec((seq, LANES), lambda hp, b: (b, hp)),
        out_shape=jax.ShapeDtypeStruct((batch * seq, hd), BF16),
        scratch_shapes=[pltpu.VMEM((ng, 2, 2, w, 2 * w), F32), pltpu.VMEM((ng, seq, LANES), F32),
                        pltpu.VMEM((ng, seq, LANES), F32)],
        compiler_params=_params(("arbitrary", "arbitrary"), 48),
    )(table, bkt, *([qkv] * (3 * ng)))


def _dil_sample_group(qkv3, cache, table, g, *, heads, head_dim, head_chunk, batch_chunk):
    bd = qkv3.shape[0]
    ng = len(DIL_GROUPS)
    hd = heads * head_dim
    window, dil = DIL_GROUPS[g]
    rows = head_chunk * head_dim
    mr = DIL_SAMPLE_ROWS
    parts = 4
    assert cache.shape[1] == window and heads % head_chunk == 0 and bd % batch_chunk == 0
    assert head_chunk in (8, 16) and rows % (parts * 8) == 0 and rows % LANES == 0
    kv_t = jnp.transpose(cache, (0, 2, 3, 4, 1)).reshape(bd, 2, hd, window)
    w_idx = np.arange(window)
    bias = jnp.where(jnp.asarray(w_idx % dil == 0)[None, :], table[_bucket_np(window - w_idx)].T, NEG_INF)
    bias0 = table[0][:, None]
    n_chunks = heads // head_chunk
    scale = head_dim ** -0.5

    def body(bias_ref, b0_ref, q_ref, kn_ref, vn_ref, *refs):
        k_refs, v_refs = refs[0:parts], refs[parts:2 * parts]
        o_ref, l_ref = refs[2 * parts:]
        h0 = _aligned(pl.program_id(1) * head_chunk, head_chunk)
        own = (lax.broadcasted_iota(I32, (mr, rows), 1) // head_dim) == lax.broadcasted_iota(I32, (mr, rows), 0)
        own_c = own[0:head_chunk]
        bias_c = bias_ref[pl.ds(h0, head_chunk), :]
        bias0_c = b0_ref[pl.ds(h0, head_chunk), :]

        def one_sequence(ib):
            q = q_ref[ib] * scale
            kt = jnp.concatenate([r[ib] for r in k_refs], axis=0).astype(BF16)
            vt = jnp.concatenate([r[ib] for r in v_refs], axis=0).astype(BF16)
            q_bd = jnp.where(own, jnp.broadcast_to(q, (mr, rows)), 0.0).astype(BF16)
            sc = _dot(q_bd, kt)[0:head_chunk] + bias_c
            s_new = jnp.sum(jnp.where(own_c, q * kn_ref[ib], 0.0), axis=1, keepdims=True) + bias0_c
            mx = jnp.maximum(jnp.max(sc, axis=1, keepdims=True), s_new)
            e = jnp.exp(sc - mx)
            e_new = jnp.exp(s_new - mx)
            den = jnp.sum(e, axis=1, keepdims=True) + e_new
            if head_chunk < mr:
                e = jnp.concatenate([e, jnp.zeros((mr - head_chunk, window), F32)], axis=0)
            pv = _dot_nt(e.astype(BF16), vt)[0:head_chunk]
            o_rows = jnp.where(own_c, (pv + e_new * vn_ref[ib]) / den, 0.0)
            o_ref[ib] = jnp.sum(o_rows, axis=0, keepdims=True)
            l_ref[ib] = jnp.sum(jnp.where(own_c, mx + jnp.log(den), 0.0), axis=0, keepdims=True)

        if batch_chunk == 1:
            one_sequence(0)
        else:
            def step(ib, carry):
                one_sequence(ib)
                return carry
            lax.fori_loop(0, batch_chunk, step, 0)

    def part_spec(which, p):
        return pl.BlockSpec((batch_chunk, None, rows // parts, window),
                            lambda b, hc: (b, which, hc * parts + p, 0))

    def row_spec(which):
        return pl.BlockSpec((batch_chunk, 1, rows), lambda b, hc: (b, 0, (which * ng + g) * n_chunks + hc))

    out_spec = pl.BlockSpec((batch_chunk, 1, rows), lambda b, hc: (b, 0, hc))
    out_sds = jax.ShapeDtypeStruct((bd, 1, hd), F32)
    return pl.pallas_call(
        body,
        grid=(bd // batch_chunk, n_chunks),
        in_specs=[pl.BlockSpec((heads, window), lambda b, hc: (0, 0)),
                  pl.BlockSpec((heads, 1), lambda b, hc: (0, 0)),
                  row_spec(0), row_spec(1), row_spec(2)]
        + [part_spec(0, p) for p in range(parts)] + [part_spec(1, p) for p in range(parts)],
        out_specs=[out_spec, out_spec],
        out_shape=[out_sds, out_sds],
        compiler_params=_params(("arbitrary", "arbitrary"), 40),
    )(bias, bias0, qkv3, qkv3, qkv3, *([kv_t] * (2 * parts)))


def _dil_sample(qkv_s, caches, table, *, heads, head_dim):
    bd = qkv_s.shape[0]
    ng = len(DIL_GROUPS)
    hd = heads * head_dim
    qkv3 = qkv_s.reshape(bd, 1, 3 * ng * hd)
    outs, lses = [], []
    for g, c in enumerate(caches):
        head_bytes = 2 * head_dim * DIL_GROUPS[g][0] * 4
        chunk = 16 if 16 * head_bytes <= DIL_SAMPLE_STEP_BYTES else 8
        seqs = max(1, DIL_SAMPLE_STEP_BYTES // (head_bytes * chunk))
        while bd % seqs:
            seqs -= 1
        o, lse = _dil_sample_group(qkv3, c, table, g, heads=heads, head_dim=head_dim, head_chunk=chunk,
                                   batch_chunk=seqs)
        outs.append(o.reshape(bd, hd))
        lses.append(lse.reshape(bd, hd))

    def body(*refs):
        o_refs, l_refs, out_ref = refs[:ng], refs[ng:2 * ng], refs[2 * ng]
        ls = [r[...] for r in l_refs]
        mx = functools.reduce(jnp.maximum, ls)
        ws = [jnp.exp(l - mx) for l in ls]
        num = functools.reduce(lambda a, b: a + b, [wt * r[...] for wt, r in zip(ws, o_refs)])
        out_ref[...] = num / functools.reduce(lambda a, b: a + b, ws)

    spec = pl.BlockSpec((bd, hd), lambda i: (0, 0))
    return pl.pallas_call(
        body,
        grid=(1,),
        in_specs=[spec] * (2 * ng),
        out_specs=spec,
        out_shape=jax.ShapeDtypeStruct((bd, hd), F32),
        compiler_params=_params(("arbitrary",), 32),
    )(*outs, *lses)


def _block_rank_select(gates, n_sel):
    sels = []
    nblk = len(gates)
    for k in range(nblk):
        rank = jnp.zeros(gates[k].shape, F32)
        for m in range(nblk):
            if m == k:
                continue
            beats = (gates[m] >= gates[k]) if m < k else (gates[m] > gates[k])
            rank = rank + beats.astype(F32)
        sels.append(rank < n_sel)
    return sels


def _moba_prompt(qkv, table, *, batch, seq, heads, kv_heads):
    d = B_HEAD_DIM
    blk = MOBA_BLOCK
    nblk = seq // blk
    grp = heads // kv_heads
    n_sel = min(MOBA_TOPK, nblk)
    scale = d ** -0.5
    r = np.arange(blk)[:, None]
    c = np.arange(blk)[None, :]
    bkt = jnp.asarray(np.stack([_bucket_np(db * blk + r - c) for db in range(nblk)]))

    def body(table_ref, bkt_ref, q_ref, k_ref, v_ref, o_ref, bias_ref, ka_ref, va_ref):
        h = pl.program_id(0)

        @pl.when(pl.program_id(1) == 0)
        def _():
            row = lax.broadcasted_iota(I32, (blk, blk), 0)
            col = lax.broadcasted_iota(I32, (blk, blk), 1)
            for db in range(nblk):
                bk = bkt_ref[db]
                t = lax.fori_loop(0, N_BUCKETS, lambda k, t: jnp.where(bk == k, table_ref[k, h], t),
                                  jnp.zeros((blk, blk), F32))
                if db == 0:
                    t = jnp.where(col > row, NEG_INF, t)
                bias_ref[db] = t

        lane = lax.broadcasted_iota(I32, (blk, d), 1)
        row16 = lax.broadcasted_iota(I32, (16, d), 0)
        kmean = jnp.zeros((16, d), F32)
        for n in range(nblk):
            rows = slice(n * blk, (n + 1) * blk)
            kn = k_ref[rows, :]
            ka_ref[rows, 0:d] = kn.astype(BF16)
            ka_ref[rows, d:2 * d] = jnp.where(lane == n, 1.0, 0.0).astype(BF16)
            va_ref[rows, 0:d] = v_ref[rows, :].astype(BF16)
            va_ref[rows, d:2 * d] = jnp.where(lane == 0, 1.0, 0.0).astype(BF16)
            kmean = jnp.where(row16 == n, jnp.mean(kn, axis=0, keepdims=True), kmean)
        k1, k2, k3 = _split3(kmean)
        row128 = lax.broadcasted_iota(I32, (LANES, blk), 0)

        for i in range(nblk):
            q = q_ref[i * blk:(i + 1) * blk, :]
            qa = (q * scale).astype(BF16)
            ranked = i > n_sel
            if ranked:
                q1, q2, q3 = _split3(q)
                gate_t = (_dot_nt(k1, q1) + _dot_nt(k1, q2) + _dot_nt(k2, q1)
                          + _dot_nt(k2, q2) + _dot_nt(k1, q3) + _dot_nt(k3, q1))
                sels = _block_rank_select([gate_t[n:n + 1, :] for n in range(i)], n_sel)
                sel_t = jnp.where(row128 == i, 1.0, 0.0)
                for n in range(i):
                    sel_t = jnp.where((row128 == n) & sels[n], 1.0, sel_t)
                sel_neg = jnp.where(sel_t.T > 0.5, 0.0, NEG_INF).astype(BF16)
                qa = jnp.concatenate([qa, sel_neg], axis=1)
            acc = None
            mx = None
            for n in range(i + 1):
                rows = slice(n * blk, (n + 1) * blk)
                kk = ka_ref[rows, :] if ranked else ka_ref[rows, 0:d]
                sc = _dot_nt(qa, kk) + bias_ref[i - n]
                blk_max = jnp.max(sc, axis=-1, keepdims=True)
                if n == 0:
                    mx = blk_max
                    acc = _dot(jnp.exp(sc - mx).astype(BF16), va_ref[rows, :])
                else:
                    m_new = jnp.maximum(mx, blk_max)
                    acc = jnp.exp(mx - m_new) * acc + _dot(jnp.exp(sc - m_new).astype(BF16), va_ref[rows, :])
                    mx = m_new
            o_ref[i * blk:(i + 1) * blk, :] = (acc[:, 0:d] / acc[:, d:d + 1]).astype(o_ref.dtype)

    return pl.pallas_call(
        body,
        grid=(heads, batch),
        in_specs=[
            pl.BlockSpec(memory_space=pltpu.SMEM),
            pl.BlockSpec((nblk, blk, blk), lambda h, b: (0, 0, 0)),
            pl.BlockSpec((seq, d), lambda h, b: (b, h)),
            pl.BlockSpec((seq, d), lambda h, b: (b, heads + h // grp)),
            pl.BlockSpec((seq, d), lambda h, b: (b, heads + kv_heads + h // grp)),
        ],
        out_specs=pl.BlockSpec((seq, d), lambda h, b: (b, h)),
        out_shape=jax.ShapeDtypeStruct((batch * seq, heads * d), BF16),
        scratch_shapes=[pltpu.VMEM((nblk, blk, blk), F32), pltpu.VMEM((seq, 2 * d), BF16),
                        pltpu.VMEM((seq, 2 * d), BF16)],
        compiler_params=_params(("arbitrary", "arbitrary"), 40),
    )(table, bkt, qkv, qkv, qkv)


def _moba_sample(q, k_new, v_new, cache, page_table, table, *, heads, kv_heads):
    bd = q.shape[0]
    d = B_HEAD_DIM
    blk = MOBA_BLOCK
    n_pages = page_table.shape[1]
    past = n_pages * PAGE_SIZE
    nblk = past // blk
    rpt = 2 * kv_heads
    assert blk == 2 * PAGE_SIZE and past % blk == 0 and rpt == 8
    grp = heads // kv_heads
    n_sel = min(MOBA_TOPK, nblk + 1)
    scale = d ** -0.5
    page_rows = PAGE_SIZE * rpt
    cols = blk * rpt
    pages = cache.reshape(cache.shape[0], page_rows, d)
    oh = np.zeros((LANES, past), np.float32)
    oh[_bucket_np(past - np.arange(past)), np.arange(past)] = 1.0
    onehot_t = jnp.asarray(oh, BF16)
    ex = np.zeros((blk, cols), np.float32)
    ex[np.arange(cols) // rpt, np.arange(cols)] = 1.0
    expand = jnp.asarray(ex, BF16)
    t_t = jnp.zeros((heads, LANES), F32).at[:, :N_BUCKETS].set(table.astype(F32).T)

    sub = 4 if nblk % 4 == 0 else 1
    psplit = 2
    part_rows = page_rows // psplit
    n_parts = sub * 2 * psplit

    def body(pt_ref, tt_ref, oh_ref, ex_ref, q_ref, kn_ref, vn_ref, *refs):
        part_refs = refs[:n_parts]
        o_ref, bias_ref, m_ref, l_ref, acc_ref, gate_ref = refs[n_parts:]
        b = pl.program_id(0)
        step = pl.program_id(1)

        @pl.when((b == 0) & (step == 0))
        def _():
            t1, t2, t3 = _split3(tt_ref[...])
            full = _dot(t1, oh_ref[...]) + _dot(t2, oh_ref[...]) + _dot(t3, oh_ref[...])
            col = lax.broadcasted_iota(I32, (heads, cols), 1)
            row = lax.broadcasted_iota(I32, (heads, cols), 0)
            own_k_row = (col % rpt) == (row // grp)
            for k in range(nblk):
                f1, f2, f3 = _split3(full[:, k * blk:(k + 1) * blk])
                bias_ref[k] = jnp.where(own_k_row, _dot(f1, ex_ref[...]) + _dot(f2, ex_ref[...])
                                        + _dot(f3, ex_ref[...]), NEG_INF)

        qv = q_ref[0]
        qb = (qv * scale).astype(BF16)
        row_grp = lax.broadcasted_iota(I32, (heads, 1), 0) // grp
        for kb in range(sub):
            n = step * sub + kb
            parts = part_refs[kb * 2 * psplit:(kb + 1) * 2 * psplit]
            x = jnp.concatenate([p[0] for p in parts], axis=0)
            xb = x.astype(BF16)
            sc = _dot_nt(qb, xb) + bias_ref[n]
            mx = jnp.max(sc, axis=1, keepdims=True)
            e = jnp.exp(sc - mx)
            den = jnp.sum(e, axis=1, keepdims=True)
            acc = _dot(pltpu.roll(e, kv_heads, axis=1).astype(BF16), xb)
            ksum = jnp.sum(x.reshape(blk, rpt, d), axis=0)
            gate = jnp.zeros((heads, 1), F32)
            for g in range(kv_heads):
                gg = jnp.sum(qv * (ksum[g:g + 1, :] * (1.0 / blk)), axis=1, keepdims=True)
                gate = jnp.where(row_grp == g, gg, gate)
            m_ref[n] = jnp.broadcast_to(mx, (heads, d))
            l_ref[n] = jnp.broadcast_to(den, (heads, d))
            gate_ref[n] = jnp.broadcast_to(gate, (heads, d))
            acc_ref[n] = acc

        @pl.when(step == nblk // sub - 1)
        def _():
            gates = [gate_ref[k] for k in range(nblk)]
            sels = _block_rank_select(gates, n_sel)
            s_own = jnp.sum(qv * kn_ref[0], axis=1, keepdims=True) * scale + tt_ref[:, 0:1]
            big = jnp.broadcast_to(s_own, (heads, d))
            for k in range(nblk):
                big = jnp.maximum(big, jnp.where(sels[k], m_ref[k], NEG_INF))
            w_own = jnp.exp(s_own - big)
            tot = w_own
            out = w_own * vn_ref[0]
            for k in range(nblk):
                wk = jnp.where(sels[k], jnp.exp(m_ref[k] - big), 0.0)
                tot = tot + wk * l_ref[k]
                out = out + wk * acc_ref[k]
            o_ref[0] = out / tot

    def part_spec(which):
        page, part = which // psplit, which % psplit
        return pl.BlockSpec((1, part_rows, d),
                            lambda b, n, pt: (pt[b * n_pages + 2 * sub * n + page], part, 0))

    row3 = pl.BlockSpec((1, heads, d), lambda b, n, pt: (b, 0, 0))
    stat = pltpu.VMEM((nblk, heads, d), F32)
    return pl.pallas_call(
        body,
        grid_spec=pltpu.PrefetchScalarGridSpec(
            num_scalar_prefetch=1,
            grid=(bd, nblk // sub),
            in_specs=[
                pl.BlockSpec((heads, LANES), lambda b, n, pt: (0, 0)),
                pl.BlockSpec((LANES, past), lambda b, n, pt: (0, 0)),
                pl.BlockSpec((blk, cols), lambda b, n, pt: (0, 0)),
                row3, row3, row3] + [part_spec(which) for which in range(n_parts)],
            out_specs=row3,
            scratch_shapes=[pltpu.VMEM((nblk, heads, cols), F32), stat, stat, stat, stat],
        ),
        out_shape=jax.ShapeDtypeStruct((bd, heads, d), F32),
        compiler_params=_params(("arbitrary", "arbitrary"), 32),
    )(page_table.reshape(-1).astype(I32), t_t, onehot_t, expand, q, k_new, v_new, *([pages] * n_parts))


def _router(x, gamma, w_router, *, tm):
    m, d = x.shape
    s = d // LANES
    ne = w_router.shape[1]
    w_pad = jnp.zeros((d, LANES), F32).at[:, :ne].set(w_router.astype(F32))

    def body(x_ref, g_ref, w_ref, h_ref, idx_ref, gate_ref):
        xv = x_ref[...]
        h = xv * lax.rsqrt(jnp.mean(xv * xv, axis=-1, keepdims=True) + RMS_EPS) * g_ref[...]
        for c in range(s):
            h_ref[pl.ds(c, tm, stride=s), :] = h[:, c * LANES:(c + 1) * LANES]
        h1, h2, h3 = _split3(h)
        w1, w2, w3 = _split3(w_ref[...])
        lg = (_dot(h1, w1) + _dot(h1, w2) + _dot(h2, w1) + _dot(h2, w2) + _dot(h1, w3) + _dot(h3, w1))
        lane = lax.broadcasted_iota(I32, lg.shape, 1)
        lane_f = lane.astype(F32)
        lg = jnp.where(lane < ne, lg, -jnp.inf)
        v1 = jnp.max(lg, axis=-1, keepdims=True)
        i1 = jnp.min(jnp.where(lg == v1, lane_f, float(LANES)), axis=-1, keepdims=True)
        lg2 = jnp.where(lane_f == i1, -jnp.inf, lg)
        v2 = jnp.max(lg2, axis=-1, keepdims=True)
        i2 = jnp.min(jnp.where(lg2 == v2, lane_f, float(LANES)), axis=-1, keepdims=True)
        e2 = jnp.exp(v2 - v1)
        g1 = 1.0 / (1.0 + e2)
        g2 = e2 / (1.0 + e2)
        idx_ref[...] = jnp.where(lane == 0, i1, jnp.where(lane == 1, i2, 0.0)).astype(I32)
        gate_ref[...] = jnp.where(lane == 0, g1, jnp.where(lane == 1, g2, 0.0))

    return pl.pallas_call(
        body,
        grid=(m // tm,),
        in_specs=[pl.BlockSpec((tm, d), lambda i: (i, 0)), pl.BlockSpec((1, d), lambda i: (0, 0)),
                  pl.BlockSpec((d, LANES), lambda i: (0, 0))],
        out_specs=[pl.BlockSpec((tm * s, LANES), lambda i: (i, 0)), pl.BlockSpec((tm, LANES), lambda i: (i, 0)),
                   pl.BlockSpec((tm, LANES), lambda i: (i, 0))],
        out_shape=[jax.ShapeDtypeStruct((m * s, LANES), F32), jax.ShapeDtypeStruct((m, LANES), I32),
                   jax.ShapeDtypeStruct((m, LANES), F32)],
        compiler_params=_params(("arbitrary",), 40),
    )(x, gamma.reshape(1, d).astype(F32), w_pad)


def _combine(x, ys_slab, dest, gamma, *, tc, n_head):
    n, d = x.shape
    s = d // LANES
    n_tiles = n // tc
    head_tiles = n_head // tc
    assert n % tc == 0 and n_head % tc == 0 and n - n_head == tc

    def body(dest_ref, x_ref, ys_ref, g_ref, oh_ref, ot_ref, buf, acc_ref, sems):
        i = pl.program_id(0)
        slot = i % 2

        def copy(tile, sl, r, k):
            row = dest_ref[2 * (tile * tc + r) + k]
            return pltpu.make_async_copy(ys_ref.at[pl.ds(_aligned(row * s, s), s), :],
                                         buf.at[sl, k, pl.ds(_aligned(r * s, s), s), :], sems.at[sl])

        def issue(tile, sl):
            def one(r, carry):
                copy(tile, sl, r, 0).start()
                copy(tile, sl, r, 1).start()
                return carry
            lax.fori_loop(0, tc, one, 0, unroll=8)

        def drain(tile, sl):
            def one(r, carry):
                copy(tile, sl, r, 0).wait()
                copy(tile, sl, r, 1).wait()
                return carry
            lax.fori_loop(0, tc, one, 0, unroll=8)

        @pl.when(i == 0)
        def _():
            issue(0, 0)

        drain(i, slot)

        @pl.when(i + 1 < n_tiles)
        def _():
            issue(i + 1, 1 - slot)

        sumsq = jnp.zeros((tc, 1), F32)
        for c in range(s):
            cols = slice(c * LANES, (c + 1) * LANES)
            xv = (x_ref[:, cols] + buf[slot, 0, pl.ds(c, tc, stride=s), :]
                  + buf[slot, 1, pl.ds(c, tc, stride=s), :])
            acc_ref[:, cols] = xv
            sumsq = sumsq + jnp.sum(xv * xv, axis=1, keepdims=True)
        y = acc_ref[...] * lax.rsqrt(sumsq * (1.0 / d) + RMS_EPS) * g_ref[...]

        @pl.when(i < head_tiles)
        def _():
            oh_ref[...] = y

        @pl.when(i >= head_tiles)
        def _():
            ot_ref[...] = y

    return pl.pallas_call(
        body,
        grid_spec=pltpu.PrefetchScalarGridSpec(
            num_scalar_prefetch=1,
            grid=(n_tiles,),
            in_specs=[pl.BlockSpec((tc, d), lambda i, dr: (i, 0)),
                      pl.BlockSpec(memory_space=pl.ANY),
                      pl.BlockSpec((1, d), lambda i, dr: (0, 0))],
            out_specs=[pl.BlockSpec((tc, d), lambda i, dr: (jnp.minimum(i, head_tiles - 1), 0)),
                       pl.BlockSpec((tc, d), lambda i, dr: (0, 0))],
            scratch_shapes=[pltpu.VMEM((2, 2, tc * s, LANES), F32), pltpu.VMEM((tc, d), F32),
                            pltpu.SemaphoreType.DMA((2,))],
        ),
        out_shape=[jax.ShapeDtypeStruct((n_head, d), F32), jax.ShapeDtypeStruct((tc, d), F32)],
        compiler_params=_params(("arbitrary",), 11 * tc * d * 4 // MIB + 8),
    )(dest, x, ys_slab, gamma.reshape(1, d).astype(F32))


def _moe(x, gamma_ffn, w_router, wg, wu, wd, gamma_final, *, tm, tf, router_tile, combine_tile, n_head):
    n, d = x.shape
    ne = wg.shape[0]
    h, idx, gates = _router(x, gamma_ffn, w_router, tm=router_tile)
    e_flat = idx[:, :TOP_K].reshape(-1)
    onehot = (e_flat[:, None] == jnp.arange(ne, dtype=I32)[None, :]).astype(I32)
    csum = jnp.cumsum(onehot, axis=0)
    rank = jnp.sum((csum - onehot) * onehot, axis=1)
    counts = csum[-1]
    padded = (counts + tm - 1) // tm * tm
    pad_end = jnp.cumsum(padded)
    dest = ((pad_end - padded)[e_flat] + rank).astype(I32)
    n_tiles = -(-(n * TOP_K + ne * (tm - 1)) // tm)
    n_rows = n_tiles * tm
    tile_expert = jnp.minimum(
        jnp.searchsorted(pad_end, jnp.arange(n_tiles, dtype=I32) * tm, side='right'), ne - 1).astype(I32)
    n_used = (pad_end[-1] // tm).astype(I32).reshape(1)
    row_tok = jnp.zeros((n_rows,), I32).at[dest].set(jnp.arange(n * TOP_K, dtype=I32) // TOP_K)
    row_gate = jnp.zeros((n_rows,), F32).at[dest].set(gates[:, :TOP_K].reshape(-1))

    s = d // LANES
    ys = _ffn(h, wg, wu, wd, tile_expert, n_used, row_gate=row_gate, row_tok=row_tok,
              add_res=False, tm=tm, tf=tf, n_rows=n_rows)
    return _combine(x, ys, dest, gamma_final, tc=combine_tile, n_head=n_head)


def kernel(x_prompt, x_sample, cache_win0_kv, cache_win1_kv, cache_win2_kv, cache_moba_kv, page_table, rel_bias, norm_mix, norm_ffn, norm_final, w_qkv_dil, w_o_dil, w_qkv_moba, w_o_moba, w_gate_dense, w_up_dense, w_down_dense, w_router, w_gate_moe, w_up_moe, w_down_moe):
    bp, sp, d = x_prompt.shape
    bd = x_sample.shape[0]
    n_p = bp * sp
    n = n_p + bd
    assert x_sample.shape[1] == 1 and n % ROW_TILE == 0
    table = rel_bias.astype(F32)
    x0 = jnp.concatenate([x_prompt.reshape(n_p, d), x_sample.reshape(bd, d)], axis=0)

    ng = len(DIL_GROUPS)
    qkv_a = _matmul(x0, w_qkv_dil[0].astype(BF16), gamma=norm_mix[0], tm=ROW_TILE, tn=1024)
    o_p = _dil_prompt(qkv_a, table, batch=bp, seq=sp, heads=A_SLOTS, head_dim=A_HEAD_DIM)
    caches = [c[0] for c in (cache_win0_kv, cache_win1_kv, cache_win2_kv)]
    o_s = _dil_sample(qkv_a[n_p:], caches, table, heads=A_SLOTS, head_dim=A_HEAD_DIM)
    o_a = jnp.concatenate([o_p, o_s.astype(BF16)], axis=0)
    x1 = _matmul(o_a, w_o_dil[0].astype(BF16), residual=x0, tm=ROW_TILE, tn=1024)
    dense_tiles = n // ROW_TILE
    x2 = _ffn(x1, w_gate_dense.astype(BF16), w_up_dense.astype(BF16), w_down_dense.astype(BF16),
              jnp.zeros((dense_tiles,), I32), jnp.full((1,), dense_tiles, I32),
              gamma=norm_ffn[0], add_res=True, tm=ROW_TILE, tf=FF_TILE)

    qd, kd = B_HEADS * B_HEAD_DIM, B_KV_HEADS * B_HEAD_DIM
    qkv_b = _matmul(x2, w_qkv_moba[0].astype(BF16), gamma=norm_mix[1], tm=ROW_TILE, tn=1024)
    m_p = _moba_prompt(qkv_b, table, batch=bp, seq=sp, heads=B_HEADS, kv_heads=B_KV_HEADS)
    qkv_bs = qkv_b[n_p:]
    grp = B_HEADS // B_KV_HEADS
    q_s = qkv_bs[:, :qd].reshape(bd, B_HEADS, B_HEAD_DIM)
    k_s = qkv_bs[:, qd:qd + kd].reshape(bd, B_KV_HEADS, B_HEAD_DIM)
    v_s = qkv_bs[:, qd + kd:].reshape(bd, B_KV_HEADS, B_HEAD_DIM)
    m_s = _moba_sample(q_s, jnp.repeat(k_s, grp, axis=1), jnp.repeat(v_s, grp, axis=1), cache_moba_kv[0],
                       page_table, table, heads=B_HEADS, kv_heads=B_KV_HEADS)
    o_b = jnp.concatenate([m_p, m_s.reshape(bd, qd).astype(BF16)], axis=0)
    x3 = _matmul(o_b, w_o_moba[0].astype(BF16), residual=x2, tm=ROW_TILE, tn=1024)
    y_p, y_s = _moe(x3, norm_ffn[1], w_router[0], w_gate_moe[0].astype(BF16), w_up_moe[0].astype(BF16),
                    w_down_moe[0].astype(BF16), norm_final,
                    tm=MOE_ROW_TILE, tf=FF_TILE, router_tile=ROW_TILE, combine_tile=bd, n_head=n_p)

    y_prompt = y_p.reshape(bp, sp, d)
    y_sample = y_s.reshape(bd, 1, d)
    hd = A_SLOTS * A_HEAD_DIM
    qa_s = qkv_a[n_p:].reshape(bd, 1, 3, ng, A_SLOTS, A_HEAD_DIM)
    win_p, win_s = [], []
    for g, (window, _) in enumerate(DIL_GROUPS):
        keep = min(window, sp)
        k_p, v_p = [qkv_a[:n_p, (which * ng + g) * hd:(which * ng + g + 1) * hd]
                    .reshape(bp, sp, A_SLOTS, A_HEAD_DIM)[:, sp - keep:] for which in (1, 2)]
        win_p.append(jnp.stack([k_p, v_p], axis=2)[None])
        win_s.append(jnp.stack([qa_s[:, :, 1, g], qa_s[:, :, 2, g]], axis=2)[None])
    kv_p = qkv_b[:n_p, qd:].reshape(bp, sp // PAGE_SIZE, PAGE_SIZE, 2, B_KV_HEADS, B_HEAD_DIM)[None]
    kv_s = qkv_bs[:, qd:].reshape(bd, 1, 2, B_KV_HEADS, B_HEAD_DIM)[None]
    return (y_prompt, y_sample, win_p[0], win_p[1], win_p[2], win_s[0], win_s[1], win_s[2], kv_p, kv_s)
```

```python
import functools
import math

import numpy as np
import jax
import jax.numpy as jnp
from jax import lax
from jax.experimental import pallas as pl
from jax.experimental.pallas import tpu as pltpu

F32 = jnp.float32
BF16 = jnp.bfloat16
I32 = jnp.int32

RMS_EPS = 1e-6
NEG_INF = -1e30
MIB = 1024 * 1024

N_BUCKETS = 32
MAX_DISTANCE = 2048
DIL_GROUPS = ((128, 1), (512, 4), (2048, 16))
A_SLOTS = 16
A_HEAD_DIM = 64
B_HEADS = 16
B_KV_HEADS = 4
B_HEAD_DIM = 128
MOBA_BLOCK = 256
MOBA_TOPK = 3
PAGE_SIZE = 128
N_EXPERTS = 8
TOP_K = 2

LANES = 128

ROW_TILE = 688
MOE_ROW_TILE = 640
FF_TILE = 512
W_SPLIT = 1
DIL_SAMPLE_STEP_BYTES = 8 * MIB
DIL_SAMPLE_ROWS = 16


def _params(sem, vmem_mib):
    return pltpu.CompilerParams(dimension_semantics=sem, vmem_limit_bytes=vmem_mib * MIB)


def _bucket_np(dist):
    max_exact = N_BUCKETS // 2
    n = np.maximum(np.asarray(dist, np.int64), 0)
    nf = np.maximum(n, 1).astype(np.float32)
    scale = np.float32((N_BUCKETS - max_exact) / math.log(MAX_DISTANCE / max_exact))
    large = max_exact + (np.log(nf / np.float32(max_exact)) * scale).astype(np.int32)
    return np.where(n < max_exact, n, np.minimum(large, N_BUCKETS - 1)).astype(np.int32)


def _split3(t):
    t1 = t.astype(BF16)
    r1 = t - t1.astype(F32)
    t2 = r1.astype(BF16)
    r2 = r1 - t2.astype(F32)
    return t1, t2, r2.astype(BF16)


def _aligned(x, m):
    return x if isinstance(x, int) else pl.multiple_of(x, m)


def _dot(a, b):
    return jnp.dot(a, b, preferred_element_type=F32)


def _dot_nt(a, b):
    return lax.dot_general(a, b, (((1,), (1,)), ((), ())), preferred_element_type=F32)


def _matmul(x, w, gamma=None, residual=None, *, tm, tn):
    m, k = x.shape
    n = w.shape[1]
    prenorm = gamma is not None
    stage = prenorm or x.dtype != BF16
    has_res = residual is not None
    kc = k // W_SPLIT

    def body(*refs):
        it = iter(refs)
        x_ref = next(it)
        g_ref = next(it) if prenorm else None
        w_refs = [next(it) for _ in range(W_SPLIT)]
        r_ref = next(it) if has_res else None
        o_ref = next(it)
        hb_ref = next(it) if stage else None
        if stage:
            @pl.when(pl.program_id(1) == 0)
            def _():
                xv = x_ref[...].astype(F32)
                if prenorm:
                    xv = xv * lax.rsqrt(jnp.mean(xv * xv, axis=-1, keepdims=True) + RMS_EPS) * g_ref[...]
                hb_ref[...] = xv.astype(BF16)
        a_ref = hb_ref if stage else x_ref
        acc = _dot(a_ref[:, 0:kc], w_refs[0][...])
        for c in range(1, W_SPLIT):
            acc = acc + _dot(a_ref[:, c * kc:(c + 1) * kc], w_refs[c][...])
        if has_res:
            acc = acc + r_ref[...]
        o_ref[...] = acc

    in_specs = [pl.BlockSpec((tm, k), lambda i, j: (i, 0))]
    args = [x]
    if prenorm:
        in_specs.append(pl.BlockSpec((1, k), lambda i, j: (0, 0)))
        args.append(gamma.reshape(1, k).astype(F32))
    for c in range(W_SPLIT):
        in_specs.append(pl.BlockSpec((kc, tn), lambda i, j, c=c: (c, j)))
        args.append(w)
    if has_res:
        in_specs.append(pl.BlockSpec((tm, tn), lambda i, j: (i, j)))
        args.append(residual)
    scratch = [pltpu.VMEM((tm, k), BF16)] if stage else []
    vmem = 2 * (tm * k * x.dtype.itemsize + k * tn * 2 + tm * tn * 4 * (2 if has_res else 1)) + tm * k * 2
    return pl.pallas_call(
        body,
        grid=(m // tm, n // tn),
        in_specs=in_specs,
        out_specs=pl.BlockSpec((tm, tn), lambda i, j: (i, j)),
        out_shape=jax.ShapeDtypeStruct((m, n), F32),
        scratch_shapes=scratch,
        compiler_params=_params(("arbitrary", "arbitrary"), vmem // MIB + 8),
    )(*args)


def _ffn(x, wg, wu, wd, tile_expert, n_used, gamma=None, row_gate=None, row_tok=None, *, add_res, tm, tf,
         n_rows=None):
    gather = row_tok is not None
    d = wg.shape[1]
    s = d // LANES
    m = n_rows if gather else x.shape[0]
    ff = wg.shape[2]
    nj = ff // tf
    n_tiles = m // tm
    prenorm = gamma is not None
    use_gate = row_gate is not None
    n_pf = 3 if gather else 2
    dc, fc = d // W_SPLIT, tf // W_SPLIT
    assert not (gather and (add_res or prenorm))

    def body(*refs):
        te_ref, nu_ref = refs[0], refs[1]
        tok_ref = refs[2] if gather else None
        it = iter(refs[n_pf:])
        x_ref = next(it)
        g_ref = next(it) if prenorm else None
        wg_refs = [next(it) for _ in range(W_SPLIT)]
        wu_refs = [next(it) for _ in range(W_SPLIT)]
        wd_refs = [next(it) for _ in range(W_SPLIT)]
        rg_ref = next(it) if use_gate else None
        o_ref = next(it)
        hb_ref = next(it)
        xg_ref, sems, acc_ref = (next(it), next(it), next(it)) if gather else (None, None, o_ref)
        i = pl.program_id(0)
        j = pl.program_id(1)
        active = i < nu_ref[0]

        def row_copy(tile, slot, r):
            tok = tok_ref[tile * tm + r]
            return pltpu.make_async_copy(
                x_ref.at[pl.ds(_aligned(tok * s, s), s), :],
                xg_ref.at[slot, pl.ds(_aligned(r * s, s), s), :],
                sems.at[slot])

        def issue(tile, slot):
            def one(r, carry):
                row_copy(tile, slot, r).start()
                return carry
            lax.fori_loop(0, tm, one, 0, unroll=8)

        def drain(tile, slot):
            def one(r, carry):
                row_copy(tile, slot, r).wait()
                return carry
            lax.fori_loop(0, tm, one, 0, unroll=8)

        @pl.when(jnp.logical_and(active, j == 0))
        def _():
            if gather:
                slot = i % 2

                @pl.when(i == 0)
                def _():
                    issue(0, 0)

                drain(i, slot)
                for c in range(s):
                    hb_ref[:, c * LANES:(c + 1) * LANES] = xg_ref[slot, pl.ds(c, tm, stride=s), :].astype(BF16)

                @pl.when(i + 1 < nu_ref[0])
                def _():
                    issue(i + 1, 1 - slot)
            else:
                xv = x_ref[...]
                if prenorm:
                    xv = xv * lax.rsqrt(jnp.mean(xv * xv, axis=-1, keepdims=True) + RMS_EPS) * g_ref[...]
                hb_ref[...] = xv.astype(BF16)

        @pl.when(j == 0)
        def _():
            acc_ref[...] = jnp.zeros_like(acc_ref)

        if gather:
            @pl.when(jnp.logical_and(j == 0, jnp.logical_not(active)))
            def _():
                o_ref[...] = jnp.zeros_like(o_ref)

        @pl.when(active)
        def _():
            gt = _dot(hb_ref[:, 0:dc], wg_refs[0][0])
            up = _dot(hb_ref[:, 0:dc], wu_refs[0][0])
            for c in range(1, W_SPLIT):
                gt = gt + _dot(hb_ref[:, c * dc:(c + 1) * dc], wg_refs[c][0])
                up = up + _dot(hb_ref[:, c * dc:(c + 1) * dc], wu_refs[c][0])
            a = (gt / (1.0 + jnp.exp(-gt)) * up).astype(BF16)
            y = _dot(a[:, 0:fc], wd_refs[0][0])
            for c in range(1, W_SPLIT):
                y = y + _dot(a[:, c * fc:(c + 1) * fc], wd_refs[c][0])
            acc_ref[...] += y

        @pl.when(jnp.logical_and(active, j == nj - 1))
        def _():
            r = acc_ref[...]
            if use_gate:
                r = r * rg_ref[...]
            if add_res:
                r = r + x_ref[...]
            if gather:
                for c in range(s):
                    o_ref[pl.ds(c, tm, stride=s), :] = r[:, c * LANES:(c + 1) * LANES]
            else:
                o_ref[...] = r

    def last_used(i, pf):
        return jnp.minimum(i, pf[1][0] - 1)

    def col_of(i, j, pf):
        return jnp.where(i < pf[1][0], j, nj - 1)

    if gather:
        in_specs = [pl.BlockSpec(memory_space=pl.ANY)]
    else:
        in_specs = [pl.BlockSpec((tm, d), lambda i, j, *pf: (last_used(i, pf), 0))]
    args = [x]
    if prenorm:
        in_specs.append(pl.BlockSpec((1, d), lambda i, j, *pf: (0, 0)))
        args.append(gamma.reshape(1, d).astype(F32))
    for wmat in (wg, wu):
        for c in range(W_SPLIT):
            in_specs.append(pl.BlockSpec(
                (1, dc, tf), lambda i, j, *pf, c=c: (pf[0][last_used(i, pf)], c, col_of(i, j, pf))))
            args.append(wmat)
    for c in range(W_SPLIT):
        in_specs.append(pl.BlockSpec(
            (1, fc, d), lambda i, j, *pf, c=c: (pf[0][last_used(i, pf)], col_of(i, j, pf) * W_SPLIT + c, 0)))
        args.append(wd)
    if use_gate:
        in_specs.append(pl.BlockSpec((tm, 1), lambda i, j, *pf: (last_used(i, pf), 0)))
        args.append(row_gate.reshape(m, 1))
    scratch = [pltpu.VMEM((tm, d), BF16)]
    vmem = 2 * (3 * d * tf * 2 + tm * d * 4 + (tm * LANES * 4 if use_gate else 0)) + tm * d * 2 + 2 * tm * d * 4
    if gather:
        scratch += [pltpu.VMEM((2, tm * s, LANES), F32), pltpu.SemaphoreType.DMA((2,)), pltpu.VMEM((tm, d), F32)]
        vmem += tm * d * 4
        out_spec = pl.BlockSpec((tm * s, LANES), lambda i, j, *pf: (i, 0))
        out_sds = jax.ShapeDtypeStruct((m * s, LANES), F32)
    else:
        out_spec = pl.BlockSpec((tm, d), lambda i, j, *pf: (i, 0))
        out_sds = jax.ShapeDtypeStruct((m, d), F32)
    prefetch = (tile_expert, n_used) + ((row_tok,) if gather else ())
    return pl.pallas_call(
        body,
        grid_spec=pltpu.PrefetchScalarGridSpec(
            num_scalar_prefetch=n_pf,
            grid=(n_tiles, nj),
            in_specs=in_specs,
            out_specs=out_spec,
            scratch_shapes=scratch,
        ),
        out_shape=out_sds,
        compiler_params=_params(("arbitrary", "arbitrary"), vmem // MIB + 10),
    )(*prefetch, *args)


def _dil_bucket_maps(w, dil):
    i = np.arange(w)[:, None]
    j = np.arange(2 * w)[None, :]
    delta = i + w - j
    valid = (delta >= 0) & (delta <= w)
    bk = np.where(valid, _bucket_np(delta * dil), -1).astype(np.int32)
    first = np.where(j >= w, bk, -1).astype(np.int32)
    return np.stack([first, bk], axis=0)


def _dil_prompt(qkv, table, *, batch, seq, heads, head_dim):
    hd = heads * head_dim
    ng = len(DIL_GROUPS)
    nhp = hd // LANES
    w = DIL_GROUPS[0][0] // DIL_GROUPS[0][1]
    assert all(win // dil == w and seq % (dil * w) == 0 for win, dil in DIL_GROUPS) and 2 * head_dim == LANES
    n_iter = seq // w
    scale = head_dim ** -0.5
    bkt = jnp.asarray(np.stack([_dil_bucket_maps(w, dil) for _, dil in DIL_GROUPS]))
    merge_rows = 256

    def body(table_ref, bkt_ref, *refs):
        q_refs, k_refs, v_refs = refs[0:ng], refs[ng:2 * ng], refs[2 * ng:3 * ng]
        o_ref, bias_ref, og_ref, lg_ref = refs[3 * ng:]
        hp = pl.program_id(0)

        @pl.when(pl.program_id(1) == 0)
        def _():
            for g in range(ng):
                for v in range(2):
                    bk = bkt_ref[g, v]
                    for half in range(2):
                        t = lax.fori_loop(
                            0, N_BUCKETS, lambda k, t: jnp.where(bk == k, table_ref[k, 2 * hp + half], t),
                            jnp.zeros(bk.shape, F32))
                        bias_ref[g, v, half] = jnp.where(bk < 0, NEG_INF, t)

        lane = lax.broadcasted_iota(I32, (w, LANES), 1)
        low = lane < head_dim
        for g, (_, dil) in enumerate(DIL_GROUPS):
            nb = seq // (dil * w)

            def one_block(idx, carry, g=g, dil=dil, nb=nb):
                r = idx // nb
                n = idx - r * nb
                if dil == 1:
                    rows_c = pl.ds(_aligned(n * w, w), w)
                    rows_p = pl.ds(_aligned(jnp.maximum(n - 1, 0) * w, w), w)
                else:
                    rows_c = pl.ds(n * (w * dil) + r, w, stride=dil)
                    rows_p = pl.ds(jnp.maximum(n - 1, 0) * (w * dil) + r, w, stride=dil)
                q2 = q_refs[g][rows_c, :] * scale
                k2 = jnp.concatenate([k_refs[g][rows_p, :], k_refs[g][rows_c, :]], axis=0).astype(BF16)
                v2 = jnp.concatenate([v_refs[g][rows_p, :], v_refs[g][rows_c, :]], axis=0).astype(BF16)
                variant = jnp.minimum(n, 1)
                outs, lses = [], []
                for half in range(2):
                    keep = low if half == 0 else jnp.logical_not(low)
                    qm = jnp.where(keep, q2, 0.0).astype(BF16)
                    sc = _dot_nt(qm, k2) + bias_ref[g, variant, half]
                    mx = jnp.max(sc, axis=-1, keepdims=True)
                    e = jnp.exp(sc - mx)
                    den = jnp.sum(e, axis=-1, keepdims=True)
                    outs.append(_dot(e.astype(BF16), v2) / den)
                    lses.append(mx + jnp.log(den))
                og_ref[g, rows_c, :] = jnp.where(low, outs[0], outs[1])
                lg_ref[g, rows_c, :] = jnp.where(low, lses[0], lses[1])
                return carry

            lax.fori_loop(0, n_iter, one_block, 0, unroll=4)

        def merge(c, carry):
            rows = pl.ds(_aligned(c * merge_rows, merge_rows), merge_rows)
            ls = [lg_ref[g, rows, :] for g in range(ng)]
            mx = functools.reduce(jnp.maximum, ls)
            ws = [jnp.exp(l - mx) for l in ls]
            num = functools.reduce(lambda a, b: a + b, [ws[g] * og_ref[g, rows, :] for g in range(ng)])
            o_ref[rows, :] = (num / functools.reduce(lambda a, b: a + b, ws)).astype(o_ref.dtype)
            return carry

        lax.fori_loop(0, seq // merge_rows, merge, 0)

    def blk(which, g):
        return pl.BlockSpec((seq, LANES), lambda hp, b: (b, (which * ng + g) * nhp + hp))

    in_specs = [pl.BlockSpec(memory_space=pltpu.SMEM),
                pl.BlockSpec((ng, 2, w, 2 * w), lambda hp, b: (0, 0, 0, 0))]
    in_specs += [blk(which, g) for which in range(3) for g in range(ng)]
    return pl.pallas_call(
        body,
        grid=(nhp, batch),
        in_specs=in_specs,
        out_specs=pl.BlockSpec((seq, LANES), lambda hp, b: (b, hp)),
        out_shape=jax.ShapeDtypeStruct((batch * seq, hd), BF16),
        scratch_shapes=[pltpu.VMEM((ng, 2, 2, w, 2 * w), F32), pltpu.VMEM((ng, seq, LANES), F32),
                        pltpu.VMEM((ng, seq, LANES), F32)],
        compiler_params=_params(("arbitrary", "arbitrary"), 48),
    )(table, bkt, *([qkv] * (3 * ng)))


def _dil_sample_group(qkv3, cache, table, g, *, heads, head_dim, head_chunk, batch_chunk):
    bd = qkv3.shape[0]
    ng = len(DIL_GROUPS)
    hd = heads * head_dim
    window, dil = DIL_GROUPS[g]
    rows = head_chunk * head_dim
    mr = DIL_SAMPLE_ROWS
    parts = 4
    assert cache.shape[1] == window and heads % head_chunk == 0 and bd % batch_chunk == 0
    assert head_chunk in (8, 16) and rows % (parts * 8) == 0 and rows % LANES == 0
    kv_t = jnp.transpose(cache, (0, 2, 3, 4, 1)).reshape(bd, 2, hd, window)
    w_idx = np.arange(window)
    bias = jnp.where(jnp.asarray(w_idx % dil == 0)[None, :], table[_bucket_np(window - w_idx)].T, NEG_INF)
    bias0 = table[0][:, None]
    n_chunks = heads // head_chunk
    scale = head_dim ** -0.5

    def body(bias_ref, b0_ref, q_ref, kn_ref, vn_ref, *refs):
        k_refs, v_refs = refs[0:parts], refs[parts:2 * parts]
        o_ref, l_ref = refs[2 * parts:]
        h0 = _aligned(pl.program_id(1) * head_chunk, head_chunk)
        own = (lax.broadcasted_iota(I32, (mr, rows), 1) // head_dim) == lax.broadcasted_iota(I32, (mr, rows), 0)
        own_c = own[0:head_chunk]
        bias_c = bias_ref[pl.ds(h0, head_chunk), :]
        bias0_c = b0_ref[pl.ds(h0, head_chunk), :]

        def one_sequence(ib):
            q = q_ref[ib] * scale
            kt = jnp.concatenate([r[ib] for r in k_refs], axis=0).astype(BF16)
            vt = jnp.concatenate([r[ib] for r in v_refs], axis=0).astype(BF16)
            q_bd = jnp.where(own, jnp.broadcast_to(q, (mr, rows)), 0.0).astype(BF16)
            sc = _dot(q_bd, kt)[0:head_chunk] + bias_c
            s_new = jnp.sum(jnp.where(own_c, q * kn_ref[ib], 0.0), axis=1, keepdims=True) + bias0_c
            mx = jnp.maximum(jnp.max(sc, axis=1, keepdims=True), s_new)
            e = jnp.exp(sc - mx)
            e_new = jnp.exp(s_new - mx)
            den = jnp.sum(e, axis=1, keepdims=True) + e_new
            if head_chunk < mr:
                e = jnp.concatenate([e, jnp.zeros((mr - head_chunk, window), F32)], axis=0)
            pv = _dot_nt(e.astype(BF16), vt)[0:head_chunk]
            o_rows = jnp.where(own_c, (pv + e_new * vn_ref[ib]) / den, 0.0)
            o_ref[ib] = jnp.sum(o_rows, axis=0, keepdims=True)
            l_ref[ib] = jnp.sum(jnp.where(own_c, mx + jnp.log(den), 0.0), axis=0, keepdims=True)

        if batch_chunk == 1:
            one_sequence(0)
        else:
            def step(ib, carry):
                one_sequence(ib)
                return carry
            lax.fori_loop(0, batch_chunk, step, 0)

    def part_spec(which, p):
        return pl.BlockSpec((batch_chunk, None, rows // parts, window),
                            lambda b, hc: (b, which, hc * parts + p, 0))

    def row_spec(which):
        return pl.BlockSpec((batch_chunk, 1, rows), lambda b, hc: (b, 0, (which * ng + g) * n_chunks + hc))

    out_spec = pl.BlockSpec((batch_chunk, 1, rows), lambda b, hc: (b, 0, hc))
    out_sds = jax.ShapeDtypeStruct((bd, 1, hd), F32)
    return pl.pallas_call(
        body,
        grid=(bd // batch_chunk, n_chunks),
        in_specs=[pl.BlockSpec((heads, window), lambda b, hc: (0, 0)),
                  pl.BlockSpec((heads, 1), lambda b, hc: (0, 0)),
                  row_spec(0), row_spec(1), row_spec(2)]
        + [part_spec(0, p) for p in range(parts)] + [part_spec(1, p) for p in range(parts)],
        out_specs=[out_spec, out_spec],
        out_shape=[out_sds, out_sds],
        compiler_params=_params(("arbitrary", "arbitrary"), 40),
    )(bias, bias0, qkv3, qkv3, qkv3, *([kv_t] * (2 * parts)))


def _dil_sample(qkv_s, caches, table, *, heads, head_dim):
    bd = qkv_s.shape[0]
    ng = len(DIL_GROUPS)
    hd = heads * head_dim
    qkv3 = qkv_s.reshape(bd, 1, 3 * ng * hd)
    outs, lses = [], []
    for g, c in enumerate(caches):
        head_bytes = 2 * head_dim * DIL_GROUPS[g][0] * 4
        chunk = 16 if 16 * head_bytes <= DIL_SAMPLE_STEP_BYTES else 8
        seqs = max(1, DIL_SAMPLE_STEP_BYTES // (head_bytes * chunk))
        while bd % seqs:
            seqs -= 1
        o, lse = _dil_sample_group(qkv3, c, table, g, heads=heads, head_dim=head_dim, head_chunk=chunk,
                                   batch_chunk=seqs)
        outs.append(o.reshape(bd, hd))
        lses.append(lse.reshape(bd, hd))

    def body(*refs):
        o_refs, l_refs, out_ref = refs[:ng], refs[ng:2 * ng], refs[2 * ng]
        ls = [r[...] for r in l_refs]
        mx = functools.reduce(jnp.maximum, ls)
        ws = [jnp.exp(l - mx) for l in ls]
        num = functools.reduce(lambda a, b: a + b, [wt * r[...] for wt, r in zip(ws, o_refs)])
        out_ref[...] = num / functools.reduce(lambda a, b: a + b, ws)

    spec = pl.BlockSpec((bd, hd), lambda i: (0, 0))
    return pl.pallas_call(
        body,
        grid=(1,),
        in_specs=[spec] * (2 * ng),
        out_specs=spec,
        out_shape=jax.ShapeDtypeStruct((bd, hd), F32),
        compiler_params=_params(("arbitrary",), 32),
    )(*outs, *lses)


def _block_rank_select(gates, n_sel):
    sels = []
    nblk = len(gates)
    for k in range(nblk):
        rank = jnp.zeros(gates[k].shape, F32)
        for m in range(nblk):
            if m == k:
                continue
            beats = (gates[m] >= gates[k]) if m < k else (gates[m] > gates[k])
            rank = rank + beats.astype(F32)
        sels.append(rank < n_sel)
    return sels


def _moba_prompt(qkv, table, *, batch, seq, heads, kv_heads):
    d = B_HEAD_DIM
    blk = MOBA_BLOCK
    nblk = seq // blk
    grp = heads // kv_heads
    n_sel = min(MOBA_TOPK, nblk)
    scale = d ** -0.5
    r = np.arange(blk)[:, None]
    c = np.arange(blk)[None, :]
    bkt = jnp.asarray(np.stack([_bucket_np(db * blk + r - c) for db in range(nblk)]))

    def body(table_ref, bkt_ref, q_ref, k_ref, v_ref, o_ref, bias_ref, ka_ref, va_ref):
        h = pl.program_id(0)

        @pl.when(pl.program_id(1) == 0)
        def _():
            row = lax.broadcasted_iota(I32, (blk, blk), 0)
            col = lax.broadcasted_iota(I32, (blk, blk), 1)
            for db in range(nblk):
                bk = bkt_ref[db]
                t = lax.fori_loop(0, N_BUCKETS, lambda k, t: jnp.where(bk == k, table_ref[k, h], t),
                                  jnp.zeros((blk, blk), F32))
                if db == 0:
                    t = jnp.where(col > row, NEG_INF, t)
                bias_ref[db] = t

        lane = lax.broadcasted_iota(I32, (blk, d), 1)
        row16 = lax.broadcasted_iota(I32, (16, d), 0)
        kmean = jnp.zeros((16, d), F32)
        for n in range(nblk):
            rows = slice(n * blk, (n + 1) * blk)
            kn = k_ref[rows, :]
            ka_ref[rows, 0:d] = kn.astype(BF16)
            ka_ref[rows, d:2 * d] = jnp.where(lane == n, 1.0, 0.0).astype(BF16)
            va_ref[rows, 0:d] = v_ref[rows, :].astype(BF16)
            va_ref[rows, d:2 * d] = jnp.where(lane == 0, 1.0, 0.0).astype(BF16)
            kmean = jnp.where(row16 == n, jnp.mean(kn, axis=0, keepdims=True), kmean)
        k1, k2, k3 = _split3(kmean)
        row128 = lax.broadcasted_iota(I32, (LANES, blk), 0)

        for i in range(nblk):
            q = q_ref[i * blk:(i + 1) * blk, :]
            qa = (q * scale).astype(BF16)
            ranked = i > n_sel
            if ranked:
                q1, q2, q3 = _split3(q)
                gate_t = (_dot_nt(k1, q1) + _dot_nt(k1, q2) + _dot_nt(k2, q1)
                          + _dot_nt(k2, q2) + _dot_nt(k1, q3) + _dot_nt(k3, q1))
                sels = _block_rank_select([gate_t[n:n + 1, :] for n in range(i)], n_sel)
                sel_t = jnp.where(row128 == i, 1.0, 0.0)
                for n in range(i):
                    sel_t = jnp.where((row128 == n) & sels[n], 1.0, sel_t)
                sel_neg = jnp.where(sel_t.T > 0.5, 0.0, NEG_INF).astype(BF16)
                qa = jnp.concatenate([qa, sel_neg], axis=1)
            acc = None
            mx = None
            for n in range(i + 1):
                rows = slice(n * blk, (n + 1) * blk)
                kk = ka_ref[rows, :] if ranked else ka_ref[rows, 0:d]
                sc = _dot_nt(qa, kk) + bias_ref[i - n]
                blk_max = jnp.max(sc, axis=-1, keepdims=True)
                if n == 0:
                    mx = blk_max
                    acc = _dot(jnp.exp(sc - mx).astype(BF16), va_ref[rows, :])
                else:
                    m_new = jnp.maximum(mx, blk_max)
                    acc = jnp.exp(mx - m_new) * acc + _dot(jnp.exp(sc - m_new).astype(BF16), va_ref[rows, :])
                    mx = m_new
            o_ref[i * blk:(i + 1) * blk, :] = (acc[:, 0:d] / acc[:, d:d + 1]).astype(o_ref.dtype)

    return pl.pallas_call(
        body,
        grid=(heads, batch),
        in_specs=[
            pl.BlockSpec(memory_space=pltpu.SMEM),
            pl.BlockSpec((nblk, blk, blk), lambda h, b: (0, 0, 0)),
            pl.BlockSpec((seq, d), lambda h, b: (b, h)),
            pl.BlockSpec((seq, d), lambda h, b: (b, heads + h // grp)),
            pl.BlockSpec((seq, d), lambda h, b: (b, heads + kv_heads + h // grp)),
        ],
        out_specs=pl.BlockSpec((seq, d), lambda h, b: (b, h)),
        out_shape=jax.ShapeDtypeStruct((batch * seq, heads * d), BF16),
        scratch_shapes=[pltpu.VMEM((nblk, blk, blk), F32), pltpu.VMEM((seq, 2 * d), BF16),
                        pltpu.VMEM((seq, 2 * d), BF16)],
        compiler_params=_params(("arbitrary", "arbitrary"), 40),
    )(table, bkt, qkv, qkv, qkv)


def _moba_sample(q, k_new, v_new, cache, page_table, table, *, heads, kv_heads):
    bd = q.shape[0]
    d = B_HEAD_DIM
    blk = MOBA_BLOCK
    n_pages = page_table.shape[1]
    past = n_pages * PAGE_SIZE
    nblk = past // blk
    rpt = 2 * kv_heads
    assert blk == 2 * PAGE_SIZE and past % blk == 0 and rpt == 8
    grp = heads // kv_heads
    n_sel = min(MOBA_TOPK, nblk + 1)
    scale = d ** -0.5
    page_rows = PAGE_SIZE * rpt
    cols = blk * rpt
    pages = cache.reshape(cache.shape[0], page_rows, d)
    oh = np.zeros((LANES, past), np.float32)
    oh[_bucket_np(past - np.arange(past)), np.arange(past)] = 1.0
    onehot_t = jnp.asarray(oh, BF16)
    ex = np.zeros((blk, cols), np.float32)
    ex[np.arange(cols) // rpt, np.arange(cols)] = 1.0
    expand = jnp.asarray(ex, BF16)
    t_t = jnp.zeros((heads, LANES), F32).at[:, :N_BUCKETS].set(table.astype(F32).T)

    sub = 4 if nblk % 4 == 0 else 1
    psplit = 2
    part_rows = page_rows // psplit
    n_parts = sub * 2 * psplit

    def body(pt_ref, tt_ref, oh_ref, ex_ref, q_ref, kn_ref, vn_ref, *refs):
        part_refs = refs[:n_parts]
        o_ref, bias_ref, m_ref, l_ref, acc_ref, gate_ref = refs[n_parts:]
        b = pl.program_id(0)
        step = pl.program_id(1)

        @pl.when((b == 0) & (step == 0))
        def _():
            t1, t2, t3 = _split3(tt_ref[...])
            full = _dot(t1, oh_ref[...]) + _dot(t2, oh_ref[...]) + _dot(t3, oh_ref[...])
            col = lax.broadcasted_iota(I32, (heads, cols), 1)
            row = lax.broadcasted_iota(I32, (heads, cols), 0)
            own_k_row = (col % rpt) == (row // grp)
            for k in range(nblk):
                f1, f2, f3 = _split3(full[:, k * blk:(k + 1) * blk])
                bias_ref[k] = jnp.where(own_k_row, _dot(f1, ex_ref[...]) + _dot(f2, ex_ref[...])
                                        + _dot(f3, ex_ref[...]), NEG_INF)

        qv = q_ref[0]
        qb = (qv * scale).astype(BF16)
        row_grp = lax.broadcasted_iota(I32, (heads, 1), 0) // grp
        for kb in range(sub):
            n = step * sub + kb
            parts = part_refs[kb * 2 * psplit:(kb + 1) * 2 * psplit]
            x = jnp.concatenate([p[0] for p in parts], axis=0)
            xb = x.astype(BF16)
            sc = _dot_nt(qb, xb) + bias_ref[n]
            mx = jnp.max(sc, axis=1, keepdims=True)
            e = jnp.exp(sc - mx)
            den = jnp.sum(e, axis=1, keepdims=True)
            acc = _dot(pltpu.roll(e, kv_heads, axis=1).astype(BF16), xb)
            ksum = jnp.sum(x.reshape(blk, rpt, d), axis=0)
            gate = jnp.zeros((heads, 1), F32)
            for g in range(kv_heads):
                gg = jnp.sum(qv * (ksum[g:g + 1, :] * (1.0 / blk)), axis=1, keepdims=True)
                gate = jnp.where(row_grp == g, gg, gate)
            m_ref[n] = jnp.broadcast_to(mx, (heads, d))
            l_ref[n] = jnp.broadcast_to(den, (heads, d))
            gate_ref[n] = jnp.broadcast_to(gate, (heads, d))
            acc_ref[n] = acc

        @pl.when(step == nblk // sub - 1)
        def _():
            gates = [gate_ref[k] for k in range(nblk)]
            sels = _block_rank_select(gates, n_sel)
            s_own = jnp.sum(qv * kn_ref[0], axis=1, keepdims=True) * scale + tt_ref[:, 0:1]
            big = jnp.broadcast_to(s_own, (heads, d))
            for k in range(nblk):
                big = jnp.maximum(big, jnp.where(sels[k], m_ref[k], NEG_INF))
            w_own = jnp.exp(s_own - big)
            tot = w_own
            out = w_own * vn_ref[0]
            for k in range(nblk):
                wk = jnp.where(sels[k], jnp.exp(m_ref[k] - big), 0.0)
                tot = tot + wk * l_ref[k]
                out = out + wk * acc_ref[k]
            o_ref[0] = out / tot

    def part_spec(which):
        page, part = which // psplit, which % psplit
        return pl.BlockSpec((1, part_rows, d),
                            lambda b, n, pt: (pt[b * n_pages + 2 * sub * n + page], part, 0))

    row3 = pl.BlockSpec((1, heads, d), lambda b, n, pt: (b, 0, 0))
    stat = pltpu.VMEM((nblk, heads, d), F32)
    return pl.pallas_call(
        body,
        grid_spec=pltpu.PrefetchScalarGridSpec(
            num_scalar_prefetch=1,
            grid=(bd, nblk // sub),
            in_specs=[
                pl.BlockSpec((heads, LANES), lambda b, n, pt: (0, 0)),
                pl.BlockSpec((LANES, past), lambda b, n, pt: (0, 0)),
                pl.BlockSpec((blk, cols), lambda b, n, pt: (0, 0)),
                row3, row3, row3] + [part_spec(which) for which in range(n_parts)],
            out_specs=row3,
            scratch_shapes=[pltpu.VMEM((nblk, heads, cols), F32), stat, stat, stat, stat],
        ),
        out_shape=jax.ShapeDtypeStruct((bd, heads, d), F32),
        compiler_params=_params(("arbitrary", "arbitrary"), 32),
    )(page_table.reshape(-1).astype(I32), t_t, onehot_t, expand, q, k_new, v_new, *([pages] * n_parts))


def _router(x, gamma, w_router, *, tm):
    m, d = x.shape
    s = d // LANES
    ne = w_router.shape[1]
    w_pad = jnp.zeros((d, LANES), F32).at[:, :ne].set(w_router.astype(F32))

    def body(x_ref, g_ref, w_ref, h_ref, idx_ref, gate_ref):
        xv = x_ref[...]
        h = xv * lax.rsqrt(jnp.mean(xv * xv, axis=-1, keepdims=True) + RMS_EPS) * g_ref[...]
        for c in range(s):
            h_ref[pl.ds(c, tm, stride=s), :] = h[:, c * LANES:(c + 1) * LANES]
        h1, h2, h3 = _split3(h)
        w1, w2, w3 = _split3(w_ref[...])
        lg = (_dot(h1, w1) + _dot(h1, w2) + _dot(h2, w1) + _dot(h2, w2) + _dot(h1, w3) + _dot(h3, w1))
        lane = lax.broadcasted_iota(I32, lg.shape, 1)
        lane_f = lane.astype(F32)
        lg = jnp.where(lane < ne, lg, -jnp.inf)
        v1 = jnp.max(lg, axis=-1, keepdims=True)
        i1 = jnp.min(jnp.where(lg == v1, lane_f, float(LANES)), axis=-1, keepdims=True)
        lg2 = jnp.where(lane_f == i1, -jnp.inf, lg)
        v2 = jnp.max(lg2, axis=-1, keepdims=True)
        i2 = jnp.min(jnp.where(lg2 == v2, lane_f, float(LANES)), axis=-1, keepdims=True)
        e2 = jnp.exp(v2 - v1)
        g1 = 1.0 / (1.0 + e2)
        g2 = e2 / (1.0 + e2)
        idx_ref[...] = jnp.where(lane == 0, i1, jnp.where(lane == 1, i2, 0.0)).astype(I32)
        gate_ref[...] = jnp.where(lane == 0, g1, jnp.where(lane == 1, g2, 0.0))

    return pl.pallas_call(
        body,
        grid=(m // tm,),
        in_specs=[pl.BlockSpec((tm, d), lambda i: (i, 0)), pl.BlockSpec((1, d), lambda i: (0, 0)),
                  pl.BlockSpec((d, LANES), lambda i: (0, 0))],
        out_specs=[pl.BlockSpec((tm * s, LANES), lambda i: (i, 0)), pl.BlockSpec((tm, LANES), lambda i: (i, 0)),
                   pl.BlockSpec((tm, LANES), lambda i: (i, 0))],
        out_shape=[jax.ShapeDtypeStruct((m * s, LANES), F32), jax.ShapeDtypeStruct((m, LANES), I32),
                   jax.ShapeDtypeStruct((m, LANES), F32)],
        compiler_params=_params(("arbitrary",), 40),
    )(x, gamma.reshape(1, d).astype(F32), w_pad)


def _combine(x, ys_slab, dest, gamma, *, tc, n_head):
    n, d = x.shape
    s = d // LANES
    n_tiles = n // tc
    head_tiles = n_head // tc
    assert n % tc == 0 and n_head % tc == 0 and n - n_head == tc

    def body(dest_ref, x_ref, ys_ref, g_ref, oh_ref, ot_ref, buf, acc_ref, sems):
        i = pl.program_id(0)
        slot = i % 2

        def copy(tile, sl, r, k):
            row = dest_ref[2 * (tile * tc + r) + k]
            return pltpu.make_async_copy(ys_ref.at[pl.ds(_aligned(row * s, s), s), :],
                                         buf.at[sl, k, pl.ds(_aligned(r * s, s), s), :], sems.at[sl])

        def issue(tile, sl):
            def one(r, carry):
                copy(tile, sl, r, 0).start()
                copy(tile, sl, r, 1).start()
                return carry
            lax.fori_loop(0, tc, one, 0, unroll=8)

        def drain(tile, sl):
            def one(r, carry):
                copy(tile, sl, r, 0).wait()
                copy(tile, sl, r, 1).wait()
                return carry
            lax.fori_loop(0, tc, one, 0, unroll=8)

        @pl.when(i == 0)
        def _():
            issue(0, 0)

        drain(i, slot)

        @pl.when(i + 1 < n_tiles)
        def _():
            issue(i + 1, 1 - slot)

        sumsq = jnp.zeros((tc, 1), F32)
        for c in range(s):
            cols = slice(c * LANES, (c + 1) * LANES)
            xv = (x_ref[:, cols] + buf[slot, 0, pl.ds(c, tc, stride=s), :]
                  + buf[slot, 1, pl.ds(c, tc, stride=s), :])
            acc_ref[:, cols] = xv
            sumsq = sumsq + jnp.sum(xv * xv, axis=1, keepdims=True)
        y = acc_ref[...] * lax.rsqrt(sumsq * (1.0 / d) + RMS_EPS) * g_ref[...]

        @pl.when(i < head_tiles)
        def _():
            oh_ref[...] = y

        @pl.when(i >= head_tiles)
        def _():
            ot_ref[...] = y

    return pl.pallas_call(
        body,
        grid_spec=pltpu.PrefetchScalarGridSpec(
            num_scalar_prefetch=1,
            grid=(n_tiles,),
            in_specs=[pl.BlockSpec((tc, d), lambda i, dr: (i, 0)),
                      pl.BlockSpec(memory_space=pl.ANY),
                      pl.BlockSpec((1, d), lambda i, dr: (0, 0))],
            out_specs=[pl.BlockSpec((tc, d), lambda i, dr: (jnp.minimum(i, head_tiles - 1), 0)),
                       pl.BlockSpec((tc, d), lambda i, dr: (0, 0))],
            scratch_shapes=[pltpu.VMEM((2, 2, tc * s, LANES), F32), pltpu.VMEM((tc, d), F32),
                            pltpu.SemaphoreType.DMA((2,))],
        ),
        out_shape=[jax.ShapeDtypeStruct((n_head, d), F32), jax.ShapeDtypeStruct((tc, d), F32)],
        compiler_params=_params(("arbitrary",), 11 * tc * d * 4 // MIB + 8),
    )(dest, x, ys_slab, gamma.reshape(1, d).astype(F32))


def _moe(x, gamma_ffn, w_router, wg, wu, wd, gamma_final, *, tm, tf, router_tile, combine_tile, n_head):
    n, d = x.shape
    ne = wg.shape[0]
    h, idx, gates = _router(x, gamma_ffn, w_router, tm=router_tile)
    e_flat = idx[:, :TOP_K].reshape(-1)
    onehot = (e_flat[:, None] == jnp.arange(ne, dtype=I32)[None, :]).astype(I32)
    csum = jnp.cumsum(onehot, axis=0)
    rank = jnp.sum((csum - onehot) * onehot, axis=1)
    counts = csum[-1]
    padded = (counts + tm - 1) // tm * tm
    pad_end = jnp.cumsum(padded)
    dest = ((pad_end - padded)[e_flat] + rank).astype(I32)
    n_tiles = -(-(n * TOP_K + ne * (tm - 1)) // tm)
    n_rows = n_tiles * tm
    tile_expert = jnp.minimum(
        jnp.searchsorted(pad_end, jnp.arange(n_tiles, dtype=I32) * tm, side='right'), ne - 1).astype(I32)
    n_used = (pad_end[-1] // tm).astype(I32).reshape(1)
    row_tok = jnp.zeros((n_rows,), I32).at[dest].set(jnp.arange(n * TOP_K, dtype=I32) // TOP_K)
    row_gate = jnp.zeros((n_rows,), F32).at[dest].set(gates[:, :TOP_K].reshape(-1))

    s = d // LANES
    ys = _ffn(h, wg, wu, wd, tile_expert, n_used, row_gate=row_gate, row_tok=row_tok,
              add_res=False, tm=tm, tf=tf, n_rows=n_rows)
    return _combine(x, ys, dest, gamma_final, tc=combine_tile, n_head=n_head)


def kernel(x_prompt, x_sample, cache_win0_kv, cache_win1_kv, cache_win2_kv, cache_moba_kv, page_table, rel_bias, norm_mix, norm_ffn, norm_final, w_qkv_dil, w_o_dil, w_qkv_moba, w_o_moba, w_gate_dense, w_up_dense, w_down_dense, w_router, w_gate_moe, w_up_moe, w_down_moe):
    bp, sp, d = x_prompt.shape
    bd = x_sample.shape[0]
    n_p = bp * sp
    n = n_p + bd
    assert x_sample.shape[1] == 1 and n % ROW_TILE == 0
    table = rel_bias.astype(F32)
    x0 = jnp.concatenate([x_prompt.reshape(n_p, d), x_sample.reshape(bd, d)], axis=0)

    ng = len(DIL_GROUPS)
    qkv_a = _matmul(x0, w_qkv_dil[0].astype(BF16), gamma=norm_mix[0], tm=ROW_TILE, tn=1024)
    o_p = _dil_prompt(qkv_a, table, batch=bp, seq=sp, heads=A_SLOTS, head_dim=A_HEAD_DIM)
    caches = [c[0] for c in (cache_win0_kv, cache_win1_kv, cache_win2_kv)]
    o_s = _dil_sample(qkv_a[n_p:], caches, table, heads=A_SLOTS, head_dim=A_HEAD_DIM)
    o_a = jnp.concatenate([o_p, o_s.astype(BF16)], axis=0)
    x1 = _matmul(o_a, w_o_dil[0].astype(BF16), residual=x0, tm=ROW_TILE, tn=1024)
    dense_tiles = n // ROW_TILE
    x2 = _ffn(x1, w_gate_dense.astype(BF16), w_up_dense.astype(BF16), w_down_dense.astype(BF16),
              jnp.zeros((dense_tiles,), I32), jnp.full((1,), dense_tiles, I32),
              gamma=norm_ffn[0], add_res=True, tm=ROW_TILE, tf=FF_TILE)

    qd, kd = B_HEADS * B_HEAD_DIM, B_KV_HEADS * B_HEAD_DIM
    qkv_b = _matmul(x2, w_qkv_moba[0].astype(BF16), gamma=norm_mix[1], tm=ROW_TILE, tn=1024)
    m_p = _moba_prompt(qkv_b, table, batch=bp, seq=sp, heads=B_HEADS, kv_heads=B_KV_HEADS)
    qkv_bs = qkv_b[n_p:]
    grp = B_HEADS // B_KV_HEADS
    q_s = qkv_bs[:, :qd].reshape(bd, B_HEADS, B_HEAD_DIM)
    k_s = qkv_bs[:, qd:qd + kd].reshape(bd, B_KV_HEADS, B_HEAD_DIM)
    v_s = qkv_bs[:, qd + kd:].reshape(bd, B_KV_HEADS, B_HEAD_DIM)
    m_s = _moba_sample(q_s, jnp.repeat(k_s, grp, axis=1), jnp.repeat(v_s, grp, axis=1), cache_moba_kv[0],
                       page_table, table, heads=B_HEADS, kv_heads=B_KV_HEADS)
    o_b = jnp.concatenate([m_p, m_s.reshape(bd, qd).astype(BF16)], axis=0)
    x3 = _matmul(o_b, w_o_moba[0].astype(BF16), residual=x2, tm=ROW_TILE, tn=1024)
    y_p, y_s = _moe(x3, norm_ffn[1], w_router[0], w_gate_moe[0].astype(BF16), w_up_moe[0].astype(BF16),
                    w_down_moe[0].astype(BF16), norm_final,
                    tm=MOE_ROW_TILE, tf=FF_TILE, router_tile=ROW_TILE, combine_tile=bd, n_head=n_p)

    y_prompt = y_p.reshape(bp, sp, d)
    y_sample = y_s.reshape(bd, 1, d)
    hd = A_SLOTS * A_HEAD_DIM
    qa_s = qkv_a[n_p:].reshape(bd, 1, 3, ng, A_SLOTS, A_HEAD_DIM)
    win_p, win_s = [], []
    for g, (window, _) in enumerate(DIL_GROUPS):
        keep = min(window, sp)
        k_p, v_p = [qkv_a[:n_p, (which * ng + g) * hd:(which * ng + g + 1) * hd]
                    .reshape(bp, sp, A_SLOTS, A_HEAD_DIM)[:, sp - keep:] for which in (1, 2)]
        win_p.append(jnp.stack([k_p, v_p], axis=2)[None])
        win_s.append(jnp.stack([qa_s[:, :, 1, g], qa_s[:, :, 2, g]], axis=2)[None])
    kv_p = qkv_b[:n_p, qd:].reshape(bp, sp // PAGE_SIZE, PAGE_SIZE, 2, B_KV_HEADS, B_HEAD_DIM)[None]
    kv_s = qkv_bs[:, qd:].reshape(bd, 1, 2, B_KV_HEADS, B_HEAD_DIM)[None]
    return (y_prompt, y_sample, win_p[0], win_p[1], win_p[2], win_s[0], win_s[1], win_s[2], kv_p, kv_s)
```
